```python
import math
import jax, jax.numpy as jnp
from jax import lax
import numpy as np

D_MODEL = 4096
BATCH = 2
SEQ = 8192
DEPTH = 4
DEC_BATCH = 8
DEC_SEQ = 2048
PAST_LEN = 128

N_META = 16
DN_H = 16
DN_HD = 128
DN_W = DN_H * DN_HD
DN_CONV = 3
CHUNK = 64
DF_H = 8
DF_HD = 128
DF_VD = 2 * DF_HD
DF_W = DF_H * DF_VD
MIX_W = DN_W + DF_W
Q_BLOCK = 128
D_FF = 11008
FFN_CONV = 3
EPS = 1e-6

IN_SIZES = (DN_W, DN_W, DN_W, DN_W,
            2 * DN_H, 2 * DN_H,
            DF_H * 2 * DF_HD, DF_H * 2 * DF_HD,
            DF_W)
IN_DIM = int(sum(IN_SIZES))
IN_SPLITS = tuple(int(s) for s in np.cumsum(IN_SIZES)[:-1])

kernel_name = "hybrid_gdn_diffattn_encoder"


def _rms_norm(x, g):
    xf = x.astype(jnp.float32)
    y = xf * lax.rsqrt(jnp.mean(xf * xf, axis=-1, keepdims=True) + EPS)
    return (y * g.astype(jnp.float32)).astype(x.dtype)


def _l2norm(x):
    return x * lax.rsqrt(jnp.sum(x * x, axis=-1, keepdims=True) + EPS)


def _dwconv_centred(x, w, b=None):
    K = w.shape[0]
    p = K // 2
    T = x.shape[1]
    xp = jnp.pad(x, ((0, 0), (p, p), (0, 0)))
    y = xp[:, 0:T] * w[0]
    for j in range(1, K):
        y = y + xp[:, j:j + T] * w[j]
    if b is not None:
        y = y + b
    return y


def _insert_pad(a):
    pad = jnp.zeros((a.shape[0], CHUNK - N_META) + a.shape[2:], a.dtype)
    return jnp.concatenate([a[:, :N_META], pad, a[:, N_META:]], axis=1)


def _gated_delta_chunked(q, k, v, g, beta):
    B, H, T, DK = q.shape
    DV = v.shape[-1]
    N = T // CHUNK

    def chunks(a):
        return a.astype(jnp.float32).reshape((B, H, N, CHUNK) + a.shape[3:])

    q = chunks(q) * (DK ** -0.5)
    k, v, g, beta = chunks(k), chunks(v), chunks(g), chunks(beta)
    g = jnp.cumsum(g, axis=-1)
    idx = jnp.arange(CHUNK)
    incl = idx[:, None] >= idx[None, :]
    strict = idx[:, None] > idx[None, :]
    gdiff = g[..., :, None] - g[..., None, :]
    decay = jnp.where(incl, jnp.exp(jnp.where(incl, gdiff, 0.0)), 0.0)
    kb = k * beta[..., None]
    a_mat = jnp.where(strict, jnp.einsum('bhncd,bhnsd->bhncs', kb, k) * decay, 0.0) \
        + jnp.eye(CHUNK, dtype=jnp.float32)
    u = lax.linalg.triangular_solve(a_mat, v * beta[..., None], left_side=True,
                                    lower=True, unit_diagonal=True)
    w = lax.linalg.triangular_solve(a_mat, kb * jnp.exp(g)[..., None], left_side=True,
                                    lower=True, unit_diagonal=True)
    attn = jnp.einsum('bhncd,bhnsd->bhncs', q, k) * decay

    def step(S, inp):
        qc, kc, uc, wc, gc, ac = inp
        v_new = uc - jnp.einsum('bhcd,bhde->bhce', wc, S)
        o = jnp.einsum('bhcd,bhde->bhce', qc * jnp.exp(gc)[..., None], S) \
            + jnp.einsum('bhcs,bhse->bhce', ac, v_new)
        g_last = gc[..., -1:]
        S = S * jnp.exp(g_last)[..., None] \
            + jnp.einsum('bhcd,bhce->bhde', kc * jnp.exp(g_last - gc)[..., None], v_new)
        return S, o

    S0 = jnp.zeros((B, H, DK, DV), jnp.float32)
    xs = tuple(jnp.moveaxis(a, 2, 0) for a in (q, k, u, w, g, attn))
    _, o = lax.scan(step, S0, xs)
    return jnp.moveaxis(o, 0, 2).reshape(B, H, T, DV)


def _gated_deltanet_bidir(z_q, z_k, z_v, z_g, z_a, z_b, conv_w, A_log, dt_bias, norm_g):
    B, T, _ = z_q.shape
    qkv = jax.nn.silu(_dwconv_centred(jnp.concatenate([z_q, z_k, z_v], axis=-1), conv_w))
    q, k, v = jnp.split(qkv, 3, axis=-1)
    q = _l2norm(q.reshape(B, T, DN_H, DN_HD).astype(jnp.float32))
    k = _l2norm(k.reshape(B, T, DN_H, DN_HD).astype(jnp.float32))
    v = v.reshape(B, T, DN_H, DN_HD).astype(jnp.float32)
    a = z_a.reshape(B, T, 2, DN_H).astype(jnp.float32)
    g = -jnp.exp(A_log.astype(jnp.float32)) * jax.nn.softplus(a + dt_bias.astype(jnp.float32))
    beta = jax.nn.sigmoid(z_b.reshape(B, T, 2, DN_H).astype(jnp.float32))
    q, k, v, g, beta = (_insert_pad(t) for t in (q, k, v, g, beta))
    qh, kh, vh = (t.transpose(0, 2, 1, 3) for t in (q, k, v))
    gh = g.transpose(0, 2, 3, 1)
    bh = beta.transpose(0, 2, 3, 1)
    o_f = _gated_delta_chunked(qh, kh, vh, gh[:, 0], bh[:, 0])
    flip = lambda t: jnp.flip(t, axis=2)
    o_b = flip(_gated_delta_chunked(flip(qh), flip(kh), flip(vh), flip(gh[:, 1]), flip(bh[:, 1])))
    o = o_f + o_b
    o = jnp.concatenate([o[:, :, :N_META], o[:, :, CHUNK:]], axis=2).transpose(0, 2, 1, 3)
    gate = jax.nn.silu(z_g.reshape(B, T, DN_H, DN_HD).astype(jnp.float32))
    o = _rms_norm(o, norm_g) * gate
    return o.reshape(B, T, DN_W).astype(z_q.dtype)


def _diff_attention(z_q, z_k, z_v, q_norm_g, k_norm_g, lam_p, subln_g, lambda_init):
    B, T, _ = z_q.shape
    q = _rms_norm(z_q.reshape(B, T, DF_H, 2, DF_HD), q_norm_g) * (DF_HD ** -0.5)
    k = _rms_norm(z_k.reshape(B, T, DF_H, 2, DF_HD), k_norm_g)
    v = z_v.reshape(B, T, DF_H, DF_VD)
    lp = lam_p.astype(jnp.float32)
    lam = jnp.exp(jnp.sum(lp[0] * lp[1])) - jnp.exp(jnp.sum(lp[2] * lp[3])) + lambda_init
    slopes = jnp.exp2(-8.0 * (jnp.arange(DF_H, dtype=jnp.float32) + 1.0) / DF_H)
    n_blk = -(-T // Q_BLOCK)
    Tq = n_blk * Q_BLOCK
    qb = jnp.pad(q, ((0, 0), (0, Tq - T), (0, 0), (0, 0), (0, 0)))
    qb = qb.reshape(B, n_blk, Q_BLOCK, DF_H, 2, DF_HD).swapaxes(0, 1)
    qpos = jnp.arange(Tq, dtype=jnp.float32).reshape(n_blk, Q_BLOCK)
    kpos = jnp.arange(T, dtype=jnp.float32)

    def attend(args):
        qblk, qp = args
        s = jnp.einsum('bqhmd,bkhmd->bhmqk', qblk, k).astype(jnp.float32)
        s = s - slopes[:, None, None, None] * jnp.abs(qp[:, None] - kpos[None, :])
        p = jax.nn.softmax(s, axis=-1)
        a = p[:, :, 0] - lam * p[:, :, 1]
        return jnp.einsum('bhqk,bkhe->bqhe', a.astype(v.dtype), v)

    o = lax.map(attend, (qb, qpos))
    o = o.swapaxes(0, 1).reshape(B, Tq, DF_H, DF_VD)[:, :T]
    o = _rms_norm(o, subln_g) * (1.0 - lambda_init)
    return o.reshape(B, T, DF_W)


def _trunk(x, meta_tokens, attn_norm, w_in, dn_conv_w, dn_A_log, dn_dt_bias, dn_norm,
           df_q_norm, df_k_norm, df_lambda, df_subln, w_out, ffn_norm, w_gate, w_up,
           ffn_conv_w, ffn_conv_b, w_down, final_norm):
    B = x.shape[0]
    meta = jnp.broadcast_to(meta_tokens.astype(x.dtype)[None], (B, N_META, D_MODEL))
    h = jnp.concatenate([meta, x], axis=1)
    for l in range(DEPTH):
        lambda_init = 0.8 - 0.6 * math.exp(-0.3 * l)
        u = _rms_norm(h, attn_norm[l])
        z = u @ w_in[l]
        zq, zk, zv, zg, za, zb, fq, fk, fv = jnp.split(z, IN_SPLITS, axis=-1)
        o_dn = _gated_deltanet_bidir(zq, zk, zv, zg, za, zb, dn_conv_w[l], dn_A_log[l],
                                     dn_dt_bias[l], dn_norm[l])
        o_df = _diff_attention(fq, fk, fv, df_q_norm[l], df_k_norm[l], df_lambda[l],
                               df_subln[l], lambda_init)
        h = h + jnp.concatenate([o_dn, o_df], axis=-1) @ w_out[l]
        u = _rms_norm(h, ffn_norm[l])
        gt = _dwconv_centred(u @ w_gate[l], ffn_conv_w[l], ffn_conv_b[l])
        h = h + (jax.nn.silu(gt) * (u @ w_up[l])) @ w_down[l]
    return _rms_norm(h, final_norm)[:, N_META:]


def setup_inputs(seed: int = 0) -> dict:
    key = jax.random.key(seed)
    ks = jax.random.split(key, 24)
    nrm = jax.random.normal
    f32 = jnp.float32
    dt = jnp.exp(jax.random.uniform(ks[6], (DEPTH, 2, DN_H), f32, math.log(1e-3), math.log(1e-1)))
    return {
        'x_prompt': nrm(ks[0], (BATCH, SEQ, D_MODEL), f32),
        'x_sample': nrm(ks[1], (DEC_BATCH, DEC_SEQ, D_MODEL), f32),
        'meta_tokens': nrm(ks[2], (N_META, D_MODEL), f32),
        'attn_norm': 1.0 + 0.02 * nrm(ks[3], (DEPTH, D_MODEL), f32),
        'w_in': nrm(ks[4], (DEPTH, D_MODEL, IN_DIM), f32) * D_MODEL ** -0.5,
        'dn_conv_w': nrm(ks[5], (DEPTH, DN_CONV, 3 * DN_W), f32) * DN_CONV ** -0.5,
        'dn_A_log': jnp.log(jax.random.uniform(ks[7], (DEPTH, 2, DN_H), f32, 1.0, 16.0)),
        'dn_dt_bias': dt + jnp.log(-jnp.expm1(-dt)),
        'dn_norm': 1.0 + 0.02 * nrm(ks[8], (DEPTH, DN_HD), f32),
        'df_q_norm': 1.0 + 0.02 * nrm(ks[9], (DEPTH, DF_HD), f32),
        'df_k_norm': 1.0 + 0.02 * nrm(ks[10], (DEPTH, DF_HD), f32),
        'df_lambda': 0.1 * nrm(ks[11], (DEPTH, 4, DF_HD), f32),
        'df_subln': 1.0 + 0.02 * nrm(ks[12], (DEPTH, DF_VD), f32),
        'w_out': nrm(ks[13], (DEPTH, MIX_W, D_MODEL), f32) * MIX_W ** -0.5,
        'ffn_norm': 1.0 + 0.02 * nrm(ks[14], (DEPTH, D_MODEL), f32),
        'w_gate': nrm(ks[15], (DEPTH, D_MODEL, D_FF), f32) * D_MODEL ** -0.5,
        'w_up': nrm(ks[16], (DEPTH, D_MODEL, D_FF), f32) * D_MODEL ** -0.5,
        'ffn_conv_w': nrm(ks[17], (DEPTH, FFN_CONV, D_FF), f32) * FFN_CONV ** -0.5,
        'ffn_conv_b': 0.01 * nrm(ks[18], (DEPTH, D_FF), f32),
        'w_down': nrm(ks[19], (DEPTH, D_FF, D_MODEL), f32) * D_FF ** -0.5,
        'final_norm': 1.0 + 0.02 * nrm(ks[20], (D_MODEL,), f32),
    }


def reference(x_prompt, x_sample, meta_tokens, attn_norm, w_in, dn_conv_w, dn_A_log, dn_dt_bias,
              dn_norm, df_q_norm, df_k_norm, df_lambda, df_subln, w_out, ffn_norm, w_gate, w_up,
              ffn_conv_w, ffn_conv_b, w_down, final_norm):
    y_prompt = _trunk(x_prompt, meta_tokens, attn_norm, w_in, dn_conv_w, dn_A_log, dn_dt_bias,
                      dn_norm, df_q_norm, df_k_norm, df_lambda, df_subln, w_out, ffn_norm,
                      w_gate, w_up, ffn_conv_w, ffn_conv_b, w_down, final_norm)
    y_sample = _trunk(x_sample, meta_tokens, attn_norm, w_in, dn_conv_w, dn_A_log, dn_dt_bias,
                      dn_norm, df_q_norm, df_k_norm, df_lambda, df_subln, w_out, ffn_norm,
                      w_gate, w_up, ffn_conv_w, ffn_conv_b, w_down, final_norm)
    return (y_prompt, y_sample)
```

```python
import functools
import math

import jax
import jax.numpy as jnp
from jax import lax
from jax.experimental import pallas as pl
from jax.experimental.pallas import tpu as pltpu

N_META = 16
DN_H = 16
DN_HD = 128
DF_H = 8
DF_HD = 128
DF_VD = 2 * DF_HD
EPS = 1e-6

FRONT = 256
PADR = FRONT - N_META
GCH = 128
INV_BASE = 16
HPG = 4
LANES = 128
HALO = 16
VMEM_LIMIT = 48 * 1024 * 1024

F32 = jnp.float32
BF16 = jnp.bfloat16


def _cparams(sem):
    return pltpu.CompilerParams(dimension_semantics=sem, vmem_limit_bytes=VMEM_LIMIT)


def _dot(a, b):
    return jnp.dot(a, b, preferred_element_type=F32)


def _dot_nt(a, b):
    return lax.dot_general(a, b, (((1,), (1,)), ((), ())), preferred_element_type=F32)


def _dot_tn(a, b):
    return lax.dot_general(a, b, (((0,), (0,)), ((), ())), preferred_element_type=F32)


def _silu(x):
    return x * (1.0 / (1.0 + jnp.exp(-x)))


def _rmsnorm_kernel(h_ref, g_ref, o_ref):
    x = h_ref[...]
    ms = jnp.mean(x * x, axis=-1, keepdims=True)
    o_ref[...] = (x * lax.rsqrt(ms + EPS) * g_ref[...]).astype(o_ref.dtype)


def rmsnorm_rows(h, g, tr):
    m, d = h.shape
    return pl.pallas_call(
        _rmsnorm_kernel,
        grid=(m // tr,),
        in_specs=[pl.BlockSpec((tr, d), lambda i: (i, 0)),
                  pl.BlockSpec((1, d), lambda i: (0, 0))],
        out_specs=pl.BlockSpec((tr, d), lambda i: (i, 0)),
        out_shape=jax.ShapeDtypeStruct((m, d), BF16),
        compiler_params=_cparams(("parallel",)),
        name="rmsnorm_rows",
    )(h, g.reshape(1, d))


def _final_norm_kernel(h_ref, g_ref, o_ref):
    x = h_ref[...]
    ms = jnp.mean(x * x, axis=-1, keepdims=True)
    o_ref[0] = x * lax.rsqrt(ms + EPS) * g_ref[...]


def final_norm_rows(h, g, batch, seq):
    m, d = h.shape
    tr = LANES
    nt = seq // tr
    nb = (FRONT + seq) // tr
    off = FRONT // tr
    return pl.pallas_call(
        _final_norm_kernel,
        grid=(batch, nt),
        in_specs=[pl.BlockSpec((tr, d), lambda b, i: (b * nb + off + i, 0)),
                  pl.BlockSpec((1, d), lambda b, i: (0, 0))],
        out_specs=pl.BlockSpec((1, tr, d), lambda b, i: (b, i, 0)),
        out_shape=jax.ShapeDtypeStruct((batch, seq, d), F32),
        compiler_params=_cparams(("parallel", "parallel")),
        name="final_norm",
    )(h, g.reshape(1, d))


def _matmul_kernel(x_ref, w_ref, o_ref):
    o_ref[...] = _dot(x_ref[...], w_ref[...]).astype(o_ref.dtype)


def matmul(x, w, tm, tn, out_dtype):
    m, k = x.shape
    n = w.shape[1]
    return pl.pallas_call(
        _matmul_kernel,
        grid=(m // tm, n // tn),
        in_specs=[pl.BlockSpec((tm, k), lambda i, j: (i, 0)),
                  pl.BlockSpec((k, tn), lambda i, j: (0, j))],
        out_specs=pl.BlockSpec((tm, tn), lambda i, j: (i, j)),
        out_shape=jax.ShapeDtypeStruct((m, n), out_dtype),
        compiler_params=_cparams(("parallel", "arbitrary")),
        name="matmul",
    )(x, w)


def _matmul_res_kernel(x_ref, w_ref, h_ref, o_ref):
    o_ref[...] = h_ref[...] + _dot(x_ref[...], w_ref[...])


def matmul_residual(x, w, h, tm, tn, kblk, nkb):
    m = x.shape[0]
    kb = x.shape[1] // nkb
    n = w.shape[1]
    return pl.pallas_call(
        _matmul_res_kernel,
        grid=(m // tm, n // tn),
        in_specs=[pl.BlockSpec((tm, kb), lambda i, j: (i, kblk)),
                  pl.BlockSpec((kb, tn), lambda i, j: (kblk, j)),
                  pl.BlockSpec((tm, tn), lambda i, j: (i, j))],
        out_specs=pl.BlockSpec((tm, tn), lambda i, j: (i, j)),
        out_shape=jax.ShapeDtypeStruct((m, n), F32),
        input_output_aliases={2: 0},
        compiler_params=_cparams(("parallel", "arbitrary")),
        name="matmul_residual",
    )(x, w, h)


def _outproj_kernel(x1_ref, x2_ref, w1_ref, w2_ref, h_ref, o_ref):
    o_ref[...] = h_ref[...] + (_dot(x1_ref[...], w1_ref[...]) + _dot(x2_ref[...], w2_ref[...]))


def outproj_residual(x1, x2, w, h, tm, tn):
    m, k1 = x1.shape
    k2 = x2.shape[1]
    assert k1 == k2 and w.shape[0] == k1 + k2
    n = w.shape[1]
    return pl.pallas_call(
        _outproj_kernel,
        grid=(m // tm, n // tn),
        in_specs=[pl.BlockSpec((tm, k1), lambda i, j: (i, 0)),
                  pl.BlockSpec((tm, k2), lambda i, j: (i, 0)),
                  pl.BlockSpec((k1, tn), lambda i, j: (0, j)),
                  pl.BlockSpec((k2, tn), lambda i, j: (1, j)),
                  pl.BlockSpec((tm, tn), lambda i, j: (i, j))],
        out_specs=pl.BlockSpec((tm, tn), lambda i, j: (i, j)),
        out_shape=jax.ShapeDtypeStruct((m, n), F32),
        input_output_aliases={4: 0},
        compiler_params=_cparams(("parallel", "arbitrary")),
        name="outproj_residual",
    )(x1, x2, w, w, h)


def _shift_rows(x, prev_row, next_row):
    t = x.shape[0]
    row = lax.broadcasted_iota(jnp.int32, x.shape, 0)
    xp = jnp.where(row == 0, prev_row, pltpu.roll(x, 1, axis=0))
    xn = jnp.where(row == t - 1, next_row, pltpu.roll(x, t - 1, axis=0))
    return xp, xn


def _ffn_gate_up_kernel(x_ref, xp_ref, xn_ref, wg_ref, wu_ref, cw_ref, cb_ref, o_ref):
    i = pl.program_id(0)
    ni = pl.num_programs(0)
    wg = wg_ref[...]
    x = x_ref[...]
    g = _dot(x, wg)
    gp = _dot(xp_ref[...], wg)[HALO - 1:HALO, :] * jnp.where(i > 0, 1.0, 0.0)
    gn = _dot(xn_ref[...], wg)[0:1, :] * jnp.where(i < ni - 1, 1.0, 0.0)
    g_prev, g_next = _shift_rows(g, gp, gn)
    cw = cw_ref[...]
    gt = g_prev * cw[0:1, :] + g * cw[1:2, :] + g_next * cw[2:3, :] + cb_ref[...]
    up = _dot(x, wu_ref[...])
    o_ref[...] = (_silu(gt) * up).astype(o_ref.dtype)


def ffn_gate_up(x, wg, wu, cw, cb, tm, tf):
    m, k = x.shape
    f = wg.shape[1]
    hb = tm // HALO
    nhb = m // HALO
    return pl.pallas_call(
        _ffn_gate_up_kernel,
        grid=(m // tm, f // tf),
        in_specs=[pl.BlockSpec((tm, k), lambda i, j: (i, 0)),
                  pl.BlockSpec((HALO, k), lambda i, j: (jnp.maximum(i * hb - 1, 0), 0)),
                  pl.BlockSpec((HALO, k), lambda i, j: (jnp.minimum((i + 1) * hb, nhb - 1), 0)),
                  pl.BlockSpec((k, tf), lambda i, j: (0, j)),
                  pl.BlockSpec((k, tf), lambda i, j: (0, j)),
                  pl.BlockSpec((3, tf), lambda i, j: (0, j)),
                  pl.BlockSpec((1, tf), lambda i, j: (0, j))],
        out_specs=pl.BlockSpec((tm, tf), lambda i, j: (i, j)),
        out_shape=jax.ShapeDtypeStruct((m, f), BF16),
        compiler_params=_cparams(("parallel", "arbitrary")),
        name="ffn_gate_up",
    )(x, x, x, wg, wu, cw, cb.reshape(1, f))


def _valid_rows(tile_idx, tiles_per_seq, tr):
    local = lax.rem(tile_idx, tiles_per_seq) * tr
    row = local + lax.broadcasted_iota(jnp.int32, (tr, 1), 0)
    return row >= PADR


def _gdn_prep_kernel(x_ref, xp_ref, xn_ref, cw_ref, o_ref, *, tiles_per_seq, n_heads):
    i = pl.program_id(0)
    j = pl.program_id(1)
    ni = pl.num_programs(0)
    x = x_ref[...].astype(F32)
    tr = x.shape[0]
    prev_row = xp_ref[HALO - 1:HALO, :].astype(F32) * jnp.where(i > 0, 1.0, 0.0)
    next_row = xn_ref[0:1, :].astype(F32) * jnp.where(i < ni - 1, 1.0, 0.0)
    x_prev, x_next = _shift_rows(x, prev_row, next_row)
    cw = cw_ref[...]
    y = _silu(x_prev * cw[0:1, :] + x * cw[1:2, :] + x_next * cw[2:3, :])
    valid = _valid_rows(i, tiles_per_seq, tr)
    is_qk = j < 2
    q_scale = jnp.where(j == 0, DN_HD ** -0.5, 1.0)
    for hh in range(n_heads):
        sl = slice(hh * DN_HD, (hh + 1) * DN_HD)
        yh = y[:, sl]
        nrm = yh * lax.rsqrt(jnp.sum(yh * yh, axis=-1, keepdims=True) + EPS) * q_scale
        yh = jnp.where(is_qk, nrm, yh)
        o_ref[:, sl] = jnp.where(valid, yh, 0.0).astype(o_ref.dtype)


def gdn_prep(z, conv_w, seq_rows, tr):
    m = z.shape[0]
    w = DN_H * DN_HD
    hb = tr // HALO
    nhb = m // HALO
    kern = functools.partial(_gdn_prep_kernel, tiles_per_seq=seq_rows // tr, n_heads=DN_H)
    return pl.pallas_call(
        kern,
        grid=(m // tr, 3),
        in_specs=[pl.BlockSpec((tr, w), lambda i, j: (i, j)),
                  pl.BlockSpec((HALO, w), lambda i, j: (jnp.maximum(i * hb - 1, 0), j)),
                  pl.BlockSpec((HALO, w), lambda i, j: (jnp.minimum((i + 1) * hb, nhb - 1), j)),
                  pl.BlockSpec((3, w), lambda i, j: (0, j))],
        out_specs=pl.BlockSpec((tr, w), lambda i, j: (i, j)),
        out_shape=jax.ShapeDtypeStruct((m, 3 * w), BF16),
        compiler_params=_cparams(("parallel", "arbitrary")),
        name="gdn_prep",
    )(z, z, z, conv_w)


def _gdn_gates_kernel(z_ref, p_ref, o_ref, *, tiles_per_seq):
    i = pl.program_id(0)
    z = z_ref[...]
    tr = z.shape[0]
    a_neg_exp = p_ref[0:1, :]
    dt_bias = p_ref[1:2, :]
    kind = p_ref[2:3, :]
    x = z + dt_bias
    softplus = jnp.maximum(x, 0.0) + jnp.log(1.0 + jnp.exp(-jnp.abs(x)))
    g = a_neg_exp * softplus
    beta = 1.0 / (1.0 + jnp.exp(-z))
    out = jnp.where(kind == 1.0, g, jnp.where(kind == 2.0, beta, 0.0))
    valid = _valid_rows(i, tiles_per_seq, tr)
    o_ref[...] = jnp.where(valid, out, 0.0)


def gdn_gates(z_ab, params, seq_rows, tr):
    m, n = z_ab.shape
    kern = functools.partial(_gdn_gates_kernel, tiles_per_seq=seq_rows // tr)
    return pl.pallas_call(
        kern,
        grid=(m // tr,),
        in_specs=[pl.BlockSpec((tr, n), lambda i: (i, 0)),
                  pl.BlockSpec((3, n), lambda i: (0, 0))],
        out_specs=pl.BlockSpec((tr, n), lambda i: (i, 0)),
        out_shape=jax.ShapeDtypeStruct((m, n), F32),
        compiler_params=_cparams(("parallel",)),
        name="gdn_gates",
    )(z_ab, params)


def _unit_triangular_inverse(a_mat, row, col):
    c = a_mat.shape[0]
    eye = (row == col).astype(F32)
    same = (row // INV_BASE) == (col // INV_BASE)
    b_bf = jnp.where(same, a_mat, 0.0).astype(BF16)
    t_mat = eye - jnp.where(same, a_mat, 0.0)
    n_lvl = int(math.log2(INV_BASE)) - 1
    if n_lvl > 0:
        p_mat = _dot(b_bf, b_bf)
    for lvl in range(n_lvl):
        p_bf = p_mat.astype(BF16)
        if lvl < n_lvl - 1:
            x = _dot(jnp.concatenate([t_mat.astype(BF16), p_bf], axis=0), p_bf)
            t_mat = t_mat + x[:c]
            p_mat = x[c:]
        else:
            t_mat = t_mat + _dot(t_mat.astype(BF16), p_bf)
    size = INV_BASE
    while size < c:
        off = ((row // (2 * size)) == (col // (2 * size))) & ((row // size) != (col // size))
        t_bf = t_mat.astype(BF16)
        y = _dot(jnp.where(off, a_mat, 0.0).astype(BF16), t_bf)
        t_mat = t_mat - _dot(t_bf, y.astype(BF16))
        size *= 2
    return t_mat


def _gdn_chunk(q, k, v, gb, s_ref, s_idx, lane_g, lane_b, cum, cum_t, reverse):
    c = q.shape[0]
    row = lax.broadcasted_iota(jnp.int32, (c, c), 0)
    col = lax.broadcasted_iota(jnp.int32, (c, c), 1)
    incl = (row <= col) if reverse else (row >= col)
    strict = (row < col) if reverse else (row > col)
    g_col = cum[:, lane_g:lane_g + 1]
    g_row = cum_t[lane_g:lane_g + 1, :]
    beta = gb[:, lane_b:lane_b + 1]
    decay = jnp.where(incl, jnp.exp(jnp.where(incl, g_col - g_row, 0.0)), 0.0)
    kf = k.astype(F32)
    kbeta = kf * beta
    e_g = jnp.exp(g_col)
    kk = _dot_nt(jnp.concatenate([kbeta.astype(BF16), q], axis=0), k)
    a_mat = jnp.where(strict, kk[:c] * decay, 0.0)
    attn = kk[c:] * decay
    t_mat = _unit_triangular_inverse(a_mat, row, col)
    rhs = jnp.concatenate([(v.astype(F32) * beta).astype(BF16), (kbeta * e_g).astype(BF16)], axis=1)
    uw = _dot(t_mat.astype(BF16), rhs)
    d = q.shape[1]
    u = uw[:, :d]
    w = uw[:, d:]
    s = s_ref[s_idx]
    qs = _dot(jnp.concatenate([(q.astype(F32) * e_g).astype(BF16), w.astype(BF16)], axis=0), s.astype(BF16))
    v_new = u - qs[c:]
    v_new_bf = v_new.astype(BF16)
    o = qs[:c] + _dot(attn.astype(BF16), v_new_bf)
    g_last = g_col[0:1, :] if reverse else g_col[c - 1:c, :]
    k_dec = (kf * jnp.exp(g_last - g_col)).astype(BF16)
    s_ref[s_idx] = s * jnp.exp(g_last) + _dot_tn(k_dec, v_new_bf)
    return o


def _gdn_scan_kernel(qf_ref, kf_ref, vf_ref, gf_ref, qb_ref, kb_ref, vb_ref, gb_ref,
                     of_ref, ob_ref, s_ref):
    @pl.when(pl.program_id(2) == 0)
    def _():
        s_ref[...] = jnp.zeros_like(s_ref)

    c = GCH
    row = lax.broadcasted_iota(jnp.int32, (c, c), 0)
    col = lax.broadcasted_iota(jnp.int32, (c, c), 1)
    for direction, (q_ref, k_ref, v_ref, g_ref, o_ref) in enumerate(
            ((qf_ref, kf_ref, vf_ref, gf_ref, of_ref), (qb_ref, kb_ref, vb_ref, gb_ref, ob_ref))):
        reverse = direction == 1
        gb = g_ref[...]
        tri = ((row <= col) if reverse else (row >= col)).astype(F32)
        cum = jnp.dot(tri, gb, preferred_element_type=F32, precision=lax.Precision.HIGHEST)
        cum_t = cum.T
        for hh in range(HPG):
            sl = slice(hh * DN_HD, (hh + 1) * DN_HD)
            o = _gdn_chunk(q_ref[:, sl], k_ref[:, sl], v_ref[:, sl], gb, s_ref,
                           direction * HPG + hh, direction * HPG + hh,
                           2 * HPG + direction * HPG + hh, cum, cum_t, reverse)
            o_ref[:, sl] = o.astype(o_ref.dtype)


def gdn_scan(qkv, gates, batch, seq_rows):
    m = qkv.shape[0]
    w = DN_H * DN_HD
    ng = DN_H // HPG
    gw = HPG * DN_HD
    nb = seq_rows // GCH

    def fwd(colblk):
        return lambda b, g, s: (b * nb + s, colblk * ng + g)

    def bwd(colblk):
        return lambda b, g, s: (b * nb + nb - 1 - s, colblk * ng + g)

    blk = (GCH, gw)
    return pl.pallas_call(
        _gdn_scan_kernel,
        grid=(batch, ng, nb),
        in_specs=[pl.BlockSpec(blk, fwd(0)), pl.BlockSpec(blk, fwd(1)), pl.BlockSpec(blk, fwd(2)),
                  pl.BlockSpec((GCH, LANES), lambda b, g, s: (b * nb + s, g)),
                  pl.BlockSpec(blk, bwd(0)), pl.BlockSpec(blk, bwd(1)), pl.BlockSpec(blk, bwd(2)),
                  pl.BlockSpec((GCH, LANES), lambda b, g, s: (b * nb + nb - 1 - s, g))],
        out_specs=[pl.BlockSpec(blk, fwd(0)), pl.BlockSpec(blk, bwd(0))],
        out_shape=[jax.ShapeDtypeStruct((m, w), BF16), jax.ShapeDtypeStruct((m, w), BF16)],
        scratch_shapes=[pltpu.VMEM((2 * HPG, DN_HD, DN_HD), F32)],
        compiler_params=_cparams(("parallel", "parallel", "arbitrary")),
        name="gdn_scan",
    )(qkv, qkv, qkv, gates, qkv, qkv, qkv, gates)


def _gdn_post_kernel(of_ref, ob_ref, gate_ref, g_ref, o_ref, *, n_heads):
    gain = g_ref[...]
    for hh in range(n_heads):
        sl = slice(hh * DN_HD, (hh + 1) * DN_HD)
        o = of_ref[:, sl].astype(F32) + ob_ref[:, sl].astype(F32)
        y = o * lax.rsqrt(jnp.mean(o * o, axis=-1, keepdims=True) + EPS) * gain
        o_ref[:, sl] = (y * _silu(gate_ref[:, sl].astype(F32))).astype(o_ref.dtype)


def gdn_post(o_f, o_b, z, gate_colblk, gain, tr):
    m, w = o_f.shape
    kern = functools.partial(_gdn_post_kernel, n_heads=DN_H)
    return pl.pallas_call(
        kern,
        grid=(m // tr,),
        in_specs=[pl.BlockSpec((tr, w), lambda i: (i, 0)),
                  pl.BlockSpec((tr, w), lambda i: (i, 0)),
                  pl.BlockSpec((tr, w), lambda i: (i, gate_colblk)),
                  pl.BlockSpec((1, DN_HD), lambda i: (0, 0))],
        out_specs=pl.BlockSpec((tr, w), lambda i: (i, 0)),
        out_shape=jax.ShapeDtypeStruct((m, w), BF16),
        compiler_params=_cparams(("parallel",)),
        name="gdn_post",
    )(o_f, o_b, z, gain.reshape(1, DN_HD))


def _qk_norm_kernel(x_ref, g_ref, o_ref, *, n_groups):
    j = pl.program_id(1)
    gain = g_ref[0]
    scale = jnp.where(j == 0, DF_HD ** -0.5, 1.0)
    for gi in range(n_groups):
        sl = slice(gi * DF_HD, (gi + 1) * DF_HD)
        x = x_ref[:, sl].astype(F32)
        y = x * lax.rsqrt(jnp.mean(x * x, axis=-1, keepdims=True) + EPS) * gain
        o_ref[:, sl] = (y * scale).astype(o_ref.dtype)


def qk_norm(z, q_colblk, gains, tr):
    m = z.shape[0]
    w = DF_H * 2 * DF_HD
    kern = functools.partial(_qk_norm_kernel, n_groups=DF_H * 2)
    return pl.pallas_call(
        kern,
        grid=(m // tr, 2),
        in_specs=[pl.BlockSpec((tr, w), lambda i, j: (i, q_colblk + j)),
                  pl.BlockSpec((1, 1, DF_HD), lambda i, j: (j, 0, 0))],
        out_specs=pl.BlockSpec((tr, w), lambda i, j: (i, j)),
        out_shape=jax.ShapeDtypeStruct((m, 2 * w), BF16),
        compiler_params=_cparams(("parallel", "arbitrary")),
        name="qk_norm",
    )(z, gains.reshape(2, 1, DF_HD))


def _attn_kernel(slopes_ref, q_ref, k_ref, v_ref, lp_ref, sg_ref, o_ref,
                 acc_ref, m_ref, l_ref, *, tk, lambda_init):
    h = pl.program_id(1)
    qi = pl.program_id(2)
    tq = q_ref.shape[0]
    nk = k_ref.shape[0] // tk
    slope = slopes_ref[h]
    q = q_ref[...]
    q_maps = (q[:, :DF_HD], q[:, DF_HD:])
    base = (qi * tq + lax.broadcasted_iota(jnp.int32, (tq, tk), 0)
            - lax.broadcasted_iota(jnp.int32, (tq, tk), 1)).astype(F32)
    key_row0 = lax.broadcasted_iota(jnp.int32, (1, tk), 1)

    acc_ref[...] = jnp.zeros_like(acc_ref)
    m_ref[...] = jnp.full_like(m_ref, -1e30)
    l_ref[...] = jnp.zeros_like(l_ref)

    def chunk(c, mask_pad):
        k0 = pl.multiple_of(c * tk, tk)
        kc = k_ref[pl.ds(k0, tk), :]
        vc = v_ref[pl.ds(k0, tk), :]
        bias = -slope * jnp.abs(base - (c * tk).astype(F32))
        for mp in range(2):
            s = _dot_nt(q_maps[mp], kc[:, mp * DF_HD:(mp + 1) * DF_HD]) + bias
            if mask_pad:
                s = jnp.where(key_row0 >= PADR, s, -1e30)
            m_old = m_ref[mp]
            m_new = jnp.maximum(m_old, jnp.max(s, axis=-1, keepdims=True))
            alpha = jnp.exp(m_old - m_new)
            p = jnp.exp(s - m_new)
            l_ref[mp] = alpha * l_ref[mp] + jnp.sum(p, axis=-1, keepdims=True)
            acc_ref[mp] = alpha * acc_ref[mp] + _dot(p.astype(BF16), vc)
            m_ref[mp] = m_new

    chunk(jnp.int32(0), True)

    def body(c, carry):
        chunk(c, False)
        return carry

    lax.fori_loop(1, nk, body, 0)

    lp = lp_ref[...]
    lam = (jnp.exp(jnp.sum(lp[0:1] * lp[1:2], axis=-1, keepdims=True))
           - jnp.exp(jnp.sum(lp[2:3] * lp[3:4], axis=-1, keepdims=True)) + lambda_init)
    o = acc_ref[0] / l_ref[0] - lam * (acc_ref[1] / l_ref[1])
    y = o * lax.rsqrt(jnp.mean(o * o, axis=-1, keepdims=True) + EPS) * sg_ref[...]
    y = y * (1.0 - lambda_init)
    q_row = qi * tq + lax.broadcasted_iota(jnp.int32, (tq, 1), 0)
    o_ref[...] = jnp.where(q_row >= PADR, y, 0.0).astype(o_ref.dtype)


def diff_attention(qk, z, v_colblk, slopes, lam_p, subln_g, lambda_init, batch, seq_rows, tq, tk):
    m = qk.shape[0]
    nq = seq_rows // tq
    hw = 2 * DF_HD
    kern = functools.partial(_attn_kernel, tk=tk, lambda_init=lambda_init)
    return pl.pallas_call(
        kern,
        grid_spec=pltpu.PrefetchScalarGridSpec(
            num_scalar_prefetch=1,
            grid=(batch, DF_H, nq),
            in_specs=[pl.BlockSpec((tq, hw), lambda b, h, i, s: (b * nq + i, h)),
                      pl.BlockSpec((seq_rows, hw), lambda b, h, i, s: (b, DF_H + h)),
                      pl.BlockSpec((seq_rows, DF_VD), lambda b, h, i, s: (b, v_colblk + h)),
                      pl.BlockSpec((4, DF_HD), lambda b, h, i, s: (0, 0)),
                      pl.BlockSpec((1, DF_VD), lambda b, h, i, s: (0, 0))],
            out_specs=pl.BlockSpec((tq, DF_VD), lambda b, h, i, s: (b * nq + i, h)),
            scratch_shapes=[pltpu.VMEM((2, tq, DF_VD), F32),
                            pltpu.VMEM((2, tq, 1), F32),
                            pltpu.VMEM((2, tq, 1), F32)]),
        out_shape=jax.ShapeDtypeStruct((m, DF_H * DF_VD), BF16),
        compiler_params=_cparams(("parallel", "parallel", "arbitrary")),
        name="diff_attention",
    )(slopes, qk, qk, z, lam_p, subln_g.reshape(1, DF_VD))


def _prep_layer_params(l, w_in, dn_A_log, dn_dt_bias, w_out, w_gate, w_up, w_down):
    dn_w = DN_H * DN_HD
    df_w = DF_H * DF_VD
    n_ab = 2 * DN_H
    ab0 = 4 * dn_w
    w = w_in[l]
    w_main = jnp.concatenate([w[:, :ab0], w[:, ab0 + 2 * n_ab:]], axis=1).astype(BF16)
    wa = w[:, ab0:ab0 + n_ab].reshape(-1, 2, DN_H // HPG, HPG)
    wb = w[:, ab0 + n_ab:ab0 + 2 * n_ab].reshape(-1, 2, DN_H // HPG, HPG)
    lanes = jnp.concatenate([wa.transpose(0, 2, 1, 3), wb.transpose(0, 2, 1, 3)], axis=2)
    lanes = lanes.reshape(w.shape[0], DN_H // HPG, 4 * HPG)
    w_ab = jnp.pad(lanes, ((0, 0), (0, 0), (0, LANES - 4 * HPG))).reshape(w.shape[0], -1).astype(BF16)

    def gate_lanes(p):
        p = p.reshape(2, DN_H // HPG, HPG).transpose(1, 0, 2).reshape(DN_H // HPG, 2 * HPG)
        return jnp.pad(p, ((0, 0), (0, LANES - 2 * HPG))).reshape(1, -1)

    kind = jnp.concatenate([jnp.full((2 * HPG,), 1.0, F32), jnp.full((2 * HPG,), 2.0, F32),
                            jnp.zeros((LANES - 4 * HPG,), F32)])
    kind = jnp.tile(kind, DN_H // HPG).reshape(1, -1)
    gate_params = jnp.concatenate([gate_lanes(-jnp.exp(dn_A_log[l].astype(F32))),
                                   gate_lanes(dn_dt_bias[l].astype(F32)), kind], axis=0)
    return dict(w_main=w_main, w_ab=w_ab, gate_params=gate_params,
                w_out=w_out[l].astype(BF16), w_gate=w_gate[l].astype(BF16),
                w_up=w_up[l].astype(BF16), w_down=w_down[l].astype(BF16))


def _pick(m, cands):
    for c in cands:
        if m % c == 0:
            return c
    raise ValueError(f"no tile for {m}")


def _trunk(x, meta_tokens, layers, norms, final_g):
    batch, seq, d = x.shape
    seq_rows = FRONT + seq
    m = batch * seq_rows
    meta = jnp.broadcast_to(meta_tokens.astype(x.dtype)[None], (batch, N_META, d))
    h = jnp.concatenate([jnp.zeros((batch, PADR, d), x.dtype), meta, x], axis=1).reshape(m, d)

    dn_w = DN_H * DN_HD
    df_qw = DF_H * 2 * DF_HD
    tm = _pick(m, (768, 512, 256))
    tr = 256
    tq = _pick(seq_rows, (768, 512, 256))
    slopes = jnp.exp2(-8.0 * (jnp.arange(DF_H, dtype=F32) + 1.0) / DF_H)
    n_main = 4 * dn_w + 2 * df_qw + DF_H * DF_VD
    tn_main = _pick(n_main, (1024, 512, 256))

    for l, (p, nm) in enumerate(zip(layers, norms)):
        lambda_init = 0.8 - 0.6 * math.exp(-0.3 * l)
        u = rmsnorm_rows(h, nm["attn_norm"], tr)
        z = matmul(u, p["w_main"], tm, tn_main, BF16)
        z_ab = matmul(u, p["w_ab"], tm, p["w_ab"].shape[1], F32)
        qkv = gdn_prep(z, nm["dn_conv_w"], seq_rows, tr)
        gates = gdn_gates(z_ab, p["gate_params"], seq_rows, tr)
        o_f, o_b = gdn_scan(qkv, gates, batch, seq_rows)
        o_dn = gdn_post(o_f, o_b, z, 3, nm["dn_norm"], tr)
        qk = qk_norm(z, (4 * dn_w) // df_qw, jnp.stack([nm["df_q_norm"], nm["df_k_norm"]]), tr)
        o_df = diff_attention(qk, z, (4 * dn_w + 2 * df_qw) // DF_VD, slopes, nm["df_lambda"],
                              nm["df_subln"], lambda_init, batch, seq_rows, tq, tq)
        h = outproj_residual(o_dn, o_df, p["w_out"], h, tm, _pick(d, (1024, 512, 256)))
        u = rmsnorm_rows(h, nm["ffn_norm"], tr)
        f = p["w_gate"].shape[1]
        act = ffn_gate_up(u, p["w_gate"], p["w_up"], nm["ffn_conv_w"], nm["ffn_conv_b"], tm,
                          _pick(f, (512, 256, 128)))
        nkb = 2 if (f // 2) % LANES == 0 else 1
        for kb in range(nkb):
            h = matmul_residual(act, p["w_down"], h, tm, _pick(d, (512, 256)), kb, nkb)
    return final_norm_rows(h, final_g, batch, seq)


def kernel(x_prompt, x_sample, meta_tokens, attn_norm, w_in, dn_conv_w, dn_A_log, dn_dt_bias, dn_norm, df_q_norm, df_k_norm, df_lambda, df_subln, w_out, ffn_norm, w_gate, w_up, ffn_conv_w, ffn_conv_b, w_down, final_norm):
    depth = w_in.shape[0]
    layers = [_prep_layer_params(l, w_in, dn_A_log, dn_dt_bias, w_out, w_gate, w_up, w_down)
              for l in range(depth)]
    norms = [dict(attn_norm=attn_norm[l], dn_conv_w=dn_conv_w[l], dn_norm=dn_norm[l],
                  df_q_norm=df_q_norm[l], df_k_norm=df_k_norm[l], df_lambda=df_lambda[l],
                  df_subln=df_subln[l], ffn_norm=ffn_norm[l], ffn_conv_w=ffn_conv_w[l],
                  ffn_conv_b=ffn_conv_b[l]) for l in range(depth)]
    y_prompt = _trunk(x_prompt, meta_tokens, layers, norms, final_norm)
    y_sample = _trunk(x_sample, meta_tokens, layers, norms, final_norm)
    return (y_prompt, y_sample)
```

```python
import functools
import math

import jax
import jax.numpy as jnp
from jax import lax
from jax.experimental import pallas as pl
from jax.experimental.pallas import tpu as pltpu

N_META = 16
DN_H = 16
DN_HD = 128
DF_H = 8
DF_HD = 128
DF_VD = 2 * DF_HD
EPS = 1e-6
LOG2E = math.log2(math.e)

FRONT = 256
PADR = FRONT - N_META
GCH = 128
INV_BASE = 16
HPG = 4
LANES = 128
HALO = 16
VMEM_LIMIT = 48 * 1024 * 1024

F32 = jnp.float32
BF16 = jnp.bfloat16


def _cparams(sem):
    return pltpu.CompilerParams(dimension_semantics=sem, vmem_limit_bytes=VMEM_LIMIT)


def _dot(a, b):
    return jnp.dot(a, b, preferred_element_type=F32)


def _dot_nt(a, b):
    return lax.dot_general(a, b, (((1,), (1,)), ((), ())), preferred_element_type=F32)


def _dot_tn(a, b):
    return lax.dot_general(a, b, (((0,), (0,)), ((), ())), preferred_element_type=F32)


def _silu(x):
    return x * (1.0 / (1.0 + jnp.exp(-x)))


def _rmsnorm_kernel(h_ref, g_ref, o_ref):
    x = h_ref[...]
    ms = jnp.mean(x * x, axis=-1, keepdims=True)
    o_ref[...] = (x * lax.rsqrt(ms + EPS) * g_ref[...]).astype(o_ref.dtype)


def rmsnorm_rows(h, g, tr):
    m, d = h.shape
    return pl.pallas_call(
        _rmsnorm_kernel,
        grid=(m // tr,),
        in_specs=[pl.BlockSpec((tr, d), lambda i: (i, 0)),
                  pl.BlockSpec((1, d), lambda i: (0, 0))],
        out_specs=pl.BlockSpec((tr, d), lambda i: (i, 0)),
        out_shape=jax.ShapeDtypeStruct((m, d), BF16),
        compiler_params=_cparams(("parallel",)),
        name="rmsnorm_rows",
    )(h, g.reshape(1, d))


def _final_norm_kernel(h_ref, g_ref, o_ref):
    x = h_ref[...]
    ms = jnp.mean(x * x, axis=-1, keepdims=True)
    o_ref[0] = x * lax.rsqrt(ms + EPS) * g_ref[...]


def final_norm_rows(h, g, batch, seq):
    m, d = h.shape
    tr = LANES
    nt = seq // tr
    nb = (FRONT + seq) // tr
    off = FRONT // tr
    return pl.pallas_call(
        _final_norm_kernel,
        grid=(batch, nt),
        in_specs=[pl.BlockSpec((tr, d), lambda b, i: (b * nb + off + i, 0)),
                  pl.BlockSpec((1, d), lambda b, i: (0, 0))],
        out_specs=pl.BlockSpec((1, tr, d), lambda b, i: (b, i, 0)),
        out_shape=jax.ShapeDtypeStruct((batch, seq, d), F32),
        compiler_params=_cparams(("parallel", "parallel")),
        name="final_norm",
    )(h, g.reshape(1, d))


def _matmul_kernel(x_ref, w_ref, o_ref):
    o_ref[...] = _dot(x_ref[...], w_ref[...]).astype(o_ref.dtype)


def matmul(x, w, tm, tn, out_dtype):
    m, k = x.shape
    n = w.shape[1]
    return pl.pallas_call(
        _matmul_kernel,
        grid=(m // tm, n // tn),
        in_specs=[pl.BlockSpec((tm, k), lambda i, j: (i, 0)),
                  pl.BlockSpec((k, tn), lambda i, j: (0, j))],
        out_specs=pl.BlockSpec((tm, tn), lambda i, j: (i, j)),
        out_shape=jax.ShapeDtypeStruct((m, n), out_dtype),
        compiler_params=_cparams(("parallel", "arbitrary")),
        name="matmul",
    )(x, w)


def _matmul_res_kernel(x_ref, w_ref, h_ref, o_ref):
    o_ref[...] = h_ref[...] + _dot(x_ref[...], w_ref[...])


def matmul_residual(x, w, h, tm, tn, kblk, nkb):
    m = x.shape[0]
    kb = x.shape[1] // nkb
    n = w.shape[1]
    return pl.pallas_call(
        _matmul_res_kernel,
        grid=(m // tm, n // tn),
        in_specs=[pl.BlockSpec((tm, kb), lambda i, j: (i, kblk)),
                  pl.BlockSpec((kb, tn), lambda i, j: (kblk, j)),
                  pl.BlockSpec((tm, tn), lambda i, j: (i, j))],
        out_specs=pl.BlockSpec((tm, tn), lambda i, j: (i, j)),
        out_shape=jax.ShapeDtypeStruct((m, n), F32),
        input_output_aliases={2: 0},
        compiler_params=_cparams(("parallel", "arbitrary")),
        name="matmul_residual",
    )(x, w, h)


def _outproj_kernel(x1_ref, x2_ref, w1_ref, w2_ref, h_ref, o_ref):
    o_ref[...] = h_ref[...] + (_dot(x1_ref[...], w1_ref[...]) + _dot(x2_ref[...], w2_ref[...]))


def outproj_residual(x1, x2, w, h, tm, tn):
    m, k1 = x1.shape
    k2 = x2.shape[1]
    assert k1 == k2 and w.shape[0] == k1 + k2
    n = w.shape[1]
    return pl.pallas_call(
        _outproj_kernel,
        grid=(m // tm, n // tn),
        in_specs=[pl.BlockSpec((tm, k1), lambda i, j: (i, 0)),
                  pl.BlockSpec((tm, k2), lambda i, j: (i, 0)),
                  pl.BlockSpec((k1, tn), lambda i, j: (0, j)),
                  pl.BlockSpec((k2, tn), lambda i, j: (1, j)),
                  pl.BlockSpec((tm, tn), lambda i, j: (i, j))],
        out_specs=pl.BlockSpec((tm, tn), lambda i, j: (i, j)),
        out_shape=jax.ShapeDtypeStruct((m, n), F32),
        input_output_aliases={4: 0},
        compiler_params=_cparams(("parallel", "arbitrary")),
        name="outproj_residual",
    )(x1, x2, w, w, h)


def _shift_rows(x, prev_row, next_row):
    t = x.shape[0]
    row = lax.broadcasted_iota(jnp.int32, x.shape, 0)
    xp = jnp.where(row == 0, prev_row, pltpu.roll(x, 1, axis=0))
    xn = jnp.where(row == t - 1, next_row, pltpu.roll(x, t - 1, axis=0))
    return xp, xn


def _ffn_gate_up_kernel(x_ref, xp_ref, xn_ref, wg_ref, wu_ref, cw_ref, cb_ref, o_ref):
    i = pl.program_id(0)
    ni = pl.num_programs(0)
    wg = wg_ref[...]
    x = x_ref[...]
    g = _dot(x, wg)
    gp = _dot(xp_ref[...], wg)[HALO - 1:HALO, :] * jnp.where(i > 0, 1.0, 0.0)
    gn = _dot(xn_ref[...], wg)[0:1, :] * jnp.where(i < ni - 1, 1.0, 0.0)
    g_prev, g_next = _shift_rows(g, gp, gn)
    cw = cw_ref[...]
    gt = g_prev * cw[0:1, :] + g * cw[1:2, :] + g_next * cw[2:3, :] + cb_ref[...]
    up = _dot(x, wu_ref[...])
    o_ref[...] = (_silu(gt) * up).astype(o_ref.dtype)


def ffn_gate_up(x, wg, wu, cw, cb, tm, tf):
    m, k = x.shape
    f = wg.shape[1]
    hb = tm // HALO
    nhb = m // HALO
    return pl.pallas_call(
        _ffn_gate_up_kernel,
        grid=(m // tm, f // tf),
        in_specs=[pl.BlockSpec((tm, k), lambda i, j: (i, 0)),
                  pl.BlockSpec((HALO, k), lambda i, j: (jnp.maximum(i * hb - 1, 0), 0)),
                  pl.BlockSpec((HALO, k), lambda i, j: (jnp.minimum((i + 1) * hb, nhb - 1), 0)),
                  pl.BlockSpec((k, tf), lambda i, j: (0, j)),
                  pl.BlockSpec((k, tf), lambda i, j: (0, j)),
                  pl.BlockSpec((3, tf), lambda i, j: (0, j)),
                  pl.BlockSpec((1, tf), lambda i, j: (0, j))],
        out_specs=pl.BlockSpec((tm, tf), lambda i, j: (i, j)),
        out_shape=jax.ShapeDtypeStruct((m, f), BF16),
        compiler_params=_cparams(("parallel", "arbitrary")),
        name="ffn_gate_up",
    )(x, x, x, wg, wu, cw, cb.reshape(1, f))


def _valid_rows(tile_idx, tiles_per_seq, tr):
    local = lax.rem(tile_idx, tiles_per_seq) * tr
    row = local + lax.broadcasted_iota(jnp.int32, (tr, 1), 0)
    return row >= PADR


def _gdn_prep_kernel(x_ref, xp_ref, xn_ref, cw_ref, o_ref, *, tiles_per_seq, n_heads):
    i = pl.program_id(0)
    j = pl.program_id(1)
    ni = pl.num_programs(0)
    x = x_ref[...].astype(F32)
    tr = x.shape[0]
    prev_row = xp_ref[HALO - 1:HALO, :].astype(F32) * jnp.where(i > 0, 1.0, 0.0)
    next_row = xn_ref[0:1, :].astype(F32) * jnp.where(i < ni - 1, 1.0, 0.0)
    x_prev, x_next = _shift_rows(x, prev_row, next_row)
    cw = cw_ref[...]
    y = _silu(x_prev * cw[0:1, :] + x * cw[1:2, :] + x_next * cw[2:3, :])
    valid = _valid_rows(i, tiles_per_seq, tr)
    is_qk = j < 2
    q_scale = jnp.where(j == 0, DN_HD ** -0.5, 1.0)
    for hh in range(n_heads):
        sl = slice(hh * DN_HD, (hh + 1) * DN_HD)
        yh = y[:, sl]
        nrm = yh * lax.rsqrt(jnp.sum(yh * yh, axis=-1, keepdims=True) + EPS) * q_scale
        yh = jnp.where(is_qk, nrm, yh)
        o_ref[:, sl] = jnp.where(valid, yh, 0.0).astype(o_ref.dtype)


def gdn_prep(z, conv_w, seq_rows, tr):
    m = z.shape[0]
    w = DN_H * DN_HD
    hb = tr // HALO
    nhb = m // HALO
    kern = functools.partial(_gdn_prep_kernel, tiles_per_seq=seq_rows // tr, n_heads=DN_H)
    return pl.pallas_call(
        kern,
        grid=(m // tr, 3),
        in_specs=[pl.BlockSpec((tr, w), lambda i, j: (i, j)),
                  pl.BlockSpec((HALO, w), lambda i, j: (jnp.maximum(i * hb - 1, 0), j)),
                  pl.BlockSpec((HALO, w), lambda i, j: (jnp.minimum((i + 1) * hb, nhb - 1), j)),
                  pl.BlockSpec((3, w), lambda i, j: (0, j))],
        out_specs=pl.BlockSpec((tr, w), lambda i, j: (i, j)),
        out_shape=jax.ShapeDtypeStruct((m, 3 * w), BF16),
        compiler_params=_cparams(("parallel", "arbitrary")),
        name="gdn_prep",
    )(z, z, z, conv_w)


def _gdn_gates_kernel(z_ref, p_ref, o_ref, *, tiles_per_seq):
    i = pl.program_id(0)
    z = z_ref[...]
    tr = z.shape[0]
    a_neg_exp = p_ref[0:1, :]
    dt_bias = p_ref[1:2, :]
    kind = p_ref[2:3, :]
    x = z + dt_bias
    softplus = jnp.maximum(x, 0.0) + jnp.log(1.0 + jnp.exp(-jnp.abs(x)))
    g = a_neg_exp * softplus
    beta = 1.0 / (1.0 + jnp.exp(-z))
    out = jnp.where(kind == 1.0, g, jnp.where(kind == 2.0, beta, 0.0))
    valid = _valid_rows(i, tiles_per_seq, tr)
    o_ref[...] = jnp.where(valid, out, 0.0)


def gdn_gates(z_ab, params, seq_rows, tr):
    m, n = z_ab.shape
    kern = functools.partial(_gdn_gates_kernel, tiles_per_seq=seq_rows // tr)
    return pl.pallas_call(
        kern,
        grid=(m // tr,),
        in_specs=[pl.BlockSpec((tr, n), lambda i: (i, 0)),
                  pl.BlockSpec((3, n), lambda i: (0, 0))],
        out_specs=pl.BlockSpec((tr, n), lambda i: (i, 0)),
        out_shape=jax.ShapeDtypeStruct((m, n), F32),
        compiler_params=_cparams(("parallel",)),
        name="gdn_gates",
    )(z_ab, params)


def _unit_triangular_inverses(a_mats, row, col):
    c = a_mats[0].shape[0]
    eye = (row == col).astype(F32)
    same = (row // INV_BASE) == (col // INV_BASE)
    diag = [jnp.where(same, a, 0.0) for a in a_mats]
    t_mats = [eye - b for b in diag]
    n_lvl = int(math.log2(INV_BASE)) - 1
    if n_lvl > 0:
        b_bf = [b.astype(BF16) for b in diag]
        p_mats = [_dot(b, b) for b in b_bf]
    for lvl in range(n_lvl):
        p_bf = [p.astype(BF16) for p in p_mats]
        if lvl < n_lvl - 1:
            xs = [_dot(jnp.concatenate([t.astype(BF16), p], axis=0), p) for t, p in zip(t_mats, p_bf)]
            t_mats = [t + x[:c] for t, x in zip(t_mats, xs)]
            p_mats = [x[c:] for x in xs]
        else:
            t_mats = [t + _dot(t.astype(BF16), p) for t, p in zip(t_mats, p_bf)]
    size = INV_BASE
    while size < c:
        off = ((row // (2 * size)) == (col // (2 * size))) & ((row // size) != (col // size))
        t_bf = [t.astype(BF16) for t in t_mats]
        ys = [_dot(jnp.where(off, a, 0.0).astype(BF16), t) for a, t in zip(a_mats, t_bf)]
        t_mats = [t - _dot(tb, y.astype(BF16)) for t, tb, y in zip(t_mats, t_bf, ys)]
        size *= 2
    return t_mats


def _gdn_scan_kernel(qf_ref, kf_ref, vf_ref, gf_ref, qb_ref, kb_ref, vb_ref, gb_ref,
                     of_ref, ob_ref, s_ref):
    @pl.when(pl.program_id(2) == 0)
    def _():
        s_ref[...] = jnp.zeros_like(s_ref)

    c = GCH
    d = DN_HD
    row = lax.broadcasted_iota(jnp.int32, (c, c), 0)
    col = lax.broadcasted_iota(jnp.int32, (c, c), 1)
    q, k, v, g_col, g_row, g_last, beta, incl, strict, outs = ([] for _ in range(10))
    for direction, (q_ref, k_ref, v_ref, g_ref, o_ref) in enumerate(
            ((qf_ref, kf_ref, vf_ref, gf_ref, of_ref), (qb_ref, kb_ref, vb_ref, gb_ref, ob_ref))):
        reverse = direction == 1
        gates = g_ref[...]
        mask_incl = (row <= col) if reverse else (row >= col)
        cum = jnp.dot(mask_incl.astype(F32), gates, preferred_element_type=F32,
                      precision=lax.Precision.HIGHEST)
        cum_t = cum.T
        for hh in range(HPG):
            sl = slice(hh * d, (hh + 1) * d)
            lane_g = direction * HPG + hh
            lane_b = 2 * HPG + lane_g
            q.append(q_ref[:, sl])
            k.append(k_ref[:, sl])
            v.append(v_ref[:, sl])
            gc = cum[:, lane_g:lane_g + 1]
            g_col.append(gc)
            g_row.append(cum_t[lane_g:lane_g + 1, :])
            g_last.append(gc[0:1, :] if reverse else gc[c - 1:c, :])
            beta.append(gates[:, lane_b:lane_b + 1])
            incl.append(mask_incl)
            strict.append((row < col) if reverse else (row > col))
            outs.append((o_ref, sl))
    n = len(q)
    rng = range(n)
    decay = [jnp.where(incl[i], jnp.exp(jnp.where(incl[i], g_col[i] - g_row[i], 0.0)), 0.0) for i in rng]
    kf = [k[i].astype(F32) for i in rng]
    kbeta = [kf[i] * beta[i] for i in rng]
    e_g = [jnp.exp(g_col[i]) for i in rng]
    kk = [_dot_nt(jnp.concatenate([kbeta[i].astype(BF16), q[i]], axis=0), k[i]) for i in rng]
    a_mats = [jnp.where(strict[i], kk[i][:c] * decay[i], 0.0) for i in rng]
    attn = [(kk[i][c:] * decay[i]).astype(BF16) for i in rng]
    t_mats = _unit_triangular_inverses(a_mats, row, col)
    rhs = [jnp.concatenate([(v[i].astype(F32) * beta[i]).astype(BF16),
                            (kbeta[i] * e_g[i]).astype(BF16)], axis=1) for i in rng]
    uw = [_dot(t_mats[i].astype(BF16), rhs[i]) for i in rng]
    s = [s_ref[i] for i in rng]
    qs = [_dot(jnp.concatenate([(q[i].astype(F32) * e_g[i]).astype(BF16), uw[i][:, d:].astype(BF16)], axis=0),
               s[i].astype(BF16)) for i in rng]
    v_new = [(uw[i][:, :d] - qs[i][c:]).astype(BF16) for i in rng]
    o = [qs[i][:c] + _dot(attn[i], v_new[i]) for i in rng]
    k_dec = [(kf[i] * jnp.exp(g_last[i] - g_col[i])).astype(BF16) for i in rng]
    s_new = [s[i] * jnp.exp(g_last[i]) + _dot_tn(k_dec[i], v_new[i]) for i in rng]
    for i in rng:
        s_ref[i] = s_new[i]
        o_ref, sl = outs[i]
        o_ref[:, sl] = o[i].astype(o_ref.dtype)


def gdn_scan(qkv, gates, batch, seq_rows):
    m = qkv.shape[0]
    w = DN_H * DN_HD
    ng = DN_H // HPG
    gw = HPG * DN_HD
    nb = seq_rows // GCH

    def fwd(colblk):
        return lambda b, g, s: (b * nb + s, colblk * ng + g)

    def bwd(colblk):
        return lambda b, g, s: (b * nb + nb - 1 - s, colblk * ng + g)

    blk = (GCH, gw)
    return pl.pallas_call(
        _gdn_scan_kernel,
        grid=(batch, ng, nb),
        in_specs=[pl.BlockSpec(blk, fwd(0)), pl.BlockSpec(blk, fwd(1)), pl.BlockSpec(blk, fwd(2)),
                  pl.BlockSpec((GCH, LANES), lambda b, g, s: (b * nb + s, g)),
                  pl.BlockSpec(blk, bwd(0)), pl.BlockSpec(blk, bwd(1)), pl.BlockSpec(blk, bwd(2)),
                  pl.BlockSpec((GCH, LANES), lambda b, g, s: (b * nb + nb - 1 - s, g))],
        out_specs=[pl.BlockSpec(blk, fwd(0)), pl.BlockSpec(blk, bwd(0))],
        out_shape=[jax.ShapeDtypeStruct((m, w), BF16), jax.ShapeDtypeStruct((m, w), BF16)],
        scratch_shapes=[pltpu.VMEM((2 * HPG, DN_HD, DN_HD), F32)],
        compiler_params=_cparams(("parallel", "parallel", "arbitrary")),
        name="gdn_scan",
    )(qkv, qkv, qkv, gates, qkv, qkv, qkv, gates)


def _gdn_post_kernel(of_ref, ob_ref, gate_ref, g_ref, o_ref, *, n_heads):
    gain = g_ref[...]
    for hh in range(n_heads):
        sl = slice(hh * DN_HD, (hh + 1) * DN_HD)
        o = of_ref[:, sl].astype(F32) + ob_ref[:, sl].astype(F32)
        y = o * lax.rsqrt(jnp.mean(o * o, axis=-1, keepdims=True) + EPS) * gain
        o_ref[:, sl] = (y * _silu(gate_ref[:, sl].astype(F32))).astype(o_ref.dtype)


def gdn_post(o_f, o_b, z, gate_colblk, gain, tr):
    m, w = o_f.shape
    kern = functools.partial(_gdn_post_kernel, n_heads=DN_H)
    return pl.pallas_call(
        kern,
        grid=(m // tr,),
        in_specs=[pl.BlockSpec((tr, w), lambda i: (i, 0)),
                  pl.BlockSpec((tr, w), lambda i: (i, 0)),
                  pl.BlockSpec((tr, w), lambda i: (i, gate_colblk)),
                  pl.BlockSpec((1, DN_HD), lambda i: (0, 0))],
        out_specs=pl.BlockSpec((tr, w), lambda i: (i, 0)),
        out_shape=jax.ShapeDtypeStruct((m, w), BF16),
        compiler_params=_cparams(("parallel",)),
        name="gdn_post",
    )(o_f, o_b, z, gain.reshape(1, DN_HD))


def _qk_norm_kernel(x_ref, g_ref, o_ref, *, n_groups):
    j = pl.program_id(1)
    gain = g_ref[0]
    scale = jnp.where(j == 0, DF_HD ** -0.5 * LOG2E, 1.0)
    for gi in range(n_groups):
        sl = slice(gi * DF_HD, (gi + 1) * DF_HD)
        x = x_ref[:, sl].astype(F32)
        y = x * lax.rsqrt(jnp.mean(x * x, axis=-1, keepdims=True) + EPS) * gain
        o_ref[:, sl] = (y * scale).astype(o_ref.dtype)


def qk_norm(z, q_colblk, gains, tr):
    m = z.shape[0]
    w = DF_H * 2 * DF_HD
    kern = functools.partial(_qk_norm_kernel, n_groups=DF_H * 2)
    return pl.pallas_call(
        kern,
        grid=(m // tr, 2),
        in_specs=[pl.BlockSpec((tr, w), lambda i, j: (i, q_colblk + j)),
                  pl.BlockSpec((1, 1, DF_HD), lambda i, j: (j, 0, 0))],
        out_specs=pl.BlockSpec((tr, w), lambda i, j: (i, j)),
        out_shape=jax.ShapeDtypeStruct((m, 2 * w), BF16),
        compiler_params=_cparams(("parallel", "arbitrary")),
        name="qk_norm",
    )(z, gains.reshape(2, 1, DF_HD))


def _attn_kernel(slopes_ref, q_ref, k_ref, v_ref, lp_ref, sg_ref, o_ref,
                 acc_ref, m_ref, l_ref, *, tk, lambda_init):
    h = pl.program_id(1)
    qi = pl.program_id(2)
    tq = q_ref.shape[0]
    nk = k_ref.shape[0] // tk
    slope2 = slopes_ref[h] * LOG2E
    q = q_ref[...]
    q_maps = (q[:, :DF_HD], q[:, DF_HD:])
    q_row = (qi * tq + lax.broadcasted_iota(jnp.int32, (tq, 1), 0)).astype(F32)
    key_off = lax.broadcasted_iota(jnp.int32, (1, tk), 1).astype(F32)
    acc_ref[...] = jnp.zeros_like(acc_ref)
    m_ref[...] = jnp.full_like(m_ref, -1e30)
    l_ref[...] = jnp.zeros_like(l_ref)

    def chunk(c, mask_pad):
        k0 = pl.multiple_of(c * tk, tk)
        kc = k_ref[pl.ds(k0, tk), :]
        vc = v_ref[pl.ds(k0, tk), :]
        k0f = k0.astype(F32)
        mixed = c == (qi * tq) // tk

        def run(key_term, row_term):
            for mp in range(2):
                x = _dot_nt(q_maps[mp], kc[:, mp * DF_HD:(mp + 1) * DF_HD]) + key_term
                if mask_pad:
                    x = jnp.where(key_off >= float(PADR), x, -1e30)
                m_old = m_ref[mp]
                m_loc = jnp.max(x, axis=-1, keepdims=True)
                if row_term is not None:
                    m_loc = m_loc + row_term
                m_new = jnp.maximum(m_old, m_loc)
                alpha = jnp.exp2(m_old - m_new)
                p = jnp.exp2(x - (m_new if row_term is None else m_new - row_term))
                l_ref[mp] = alpha * l_ref[mp] + jnp.sum(p, axis=-1, keepdims=True)
                m_ref[mp] = m_new
                acc_ref[mp] = alpha * acc_ref[mp] + _dot(p.astype(BF16), vc)

        @pl.when(mixed)
        def _():
            run(-slope2 * jnp.abs(q_row - (k0f + key_off)), None)

        @pl.when(jnp.logical_not(mixed))
        def _():
            sgn = jnp.where(k0 < qi * tq, slope2, -slope2)
            run(sgn * key_off, -sgn * (q_row - k0f))

    chunk(jnp.int32(0), True)

    def body(c, carry):
        chunk(c, False)
        return carry

    lax.fori_loop(1, nk, body, 0)

    lp = lp_ref[...]
    lam = (jnp.exp(jnp.sum(lp[0:1] * lp[1:2], axis=-1, keepdims=True))
           - jnp.exp(jnp.sum(lp[2:3] * lp[3:4], axis=-1, keepdims=True)) + lambda_init)
    o = acc_ref[0] / l_ref[0] - lam * (acc_ref[1] / l_ref[1])
    y = o * lax.rsqrt(jnp.mean(o * o, axis=-1, keepdims=True) + EPS) * sg_ref[...]
    y = y * (1.0 - lambda_init)
    o_ref[...] = jnp.where(q_row >= float(PADR), y, 0.0).astype(o_ref.dtype)


def diff_attention(qk, z, v_colblk, slopes, lam_p, subln_g, lambda_init, batch, seq_rows, tq, tk):
    m = qk.shape[0]
    nq = seq_rows // tq
    hw = 2 * DF_HD
    assert tk % tq == 0 and seq_rows % tk == 0 and tk > PADR
    kern = functools.partial(_attn_kernel, tk=tk, lambda_init=lambda_init)
    return pl.pallas_call(
        kern,
        grid_spec=pltpu.PrefetchScalarGridSpec(
            num_scalar_prefetch=1,
            grid=(batch, DF_H, nq),
            in_specs=[pl.BlockSpec((tq, hw), lambda b, h, i, s: (b * nq + i, h)),
                      pl.BlockSpec((seq_rows, hw), lambda b, h, i, s: (b, DF_H + h)),
                      pl.BlockSpec((seq_rows, DF_VD), lambda b, h, i, s: (b, v_colblk + h)),
                      pl.BlockSpec((4, DF_HD), lambda b, h, i, s: (0, 0)),
                      pl.BlockSpec((1, DF_VD), lambda b, h, i, s: (0, 0))],
            out_specs=pl.BlockSpec((tq, DF_VD), lambda b, h, i, s: (b * nq + i, h)),
            scratch_shapes=[pltpu.VMEM((2, tq, DF_VD), F32),
                            pltpu.VMEM((2, tq, 1), F32),
                            pltpu.VMEM((2, tq, 1), F32)]),
        out_shape=jax.ShapeDtypeStruct((m, DF_H * DF_VD), BF16),
        compiler_params=_cparams(("parallel", "parallel", "arbitrary")),
        name="diff_attention",
    )(slopes, qk, qk, z, lam_p, subln_g.reshape(1, DF_VD))


def _prep_layer_params(l, w_in, dn_A_log, dn_dt_bias, w_out, w_gate, w_up, w_down):
    dn_w = DN_H * DN_HD
    df_w = DF_H * DF_VD
    n_ab = 2 * DN_H
    ab0 = 4 * dn_w
    w = w_in[l]
    w_main = jnp.concatenate([w[:, :ab0], w[:, ab0 + 2 * n_ab:]], axis=1).astype(BF16)
    wa = w[:, ab0:ab0 + n_ab].reshape(-1, 2, DN_H // HPG, HPG)
    wb = w[:, ab0 + n_ab:ab0 + 2 * n_ab].reshape(-1, 2, DN_H // HPG, HPG)
    lanes = jnp.concatenate([wa.transpose(0, 2, 1, 3), wb.transpose(0, 2, 1, 3)], axis=2)
    lanes = lanes.reshape(w.shape[0], DN_H // HPG, 4 * HPG)
    w_ab = jnp.pad(lanes, ((0, 0), (0, 0), (0, LANES - 4 * HPG))).reshape(w.shape[0], -1).astype(BF16)

    def gate_lanes(p):
        p = p.reshape(2, DN_H // HPG, HPG).transpose(1, 0, 2).reshape(DN_H // HPG, 2 * HPG)
        return jnp.pad(p, ((0, 0), (0, LANES - 2 * HPG))).reshape(1, -1)

    kind = jnp.concatenate([jnp.full((2 * HPG,), 1.0, F32), jnp.full((2 * HPG,), 2.0, F32),
                            jnp.zeros((LANES - 4 * HPG,), F32)])
    kind = jnp.tile(kind, DN_H // HPG).reshape(1, -1)
    gate_params = jnp.concatenate([gate_lanes(-jnp.exp(dn_A_log[l].astype(F32))),
                                   gate_lanes(dn_dt_bias[l].astype(F32)), kind], axis=0)
    return dict(w_main=w_main, w_ab=w_ab, gate_params=gate_params,
                w_out=w_out[l].astype(BF16), w_gate=w_gate[l].astype(BF16),
                w_up=w_up[l].astype(BF16), w_down=w_down[l].astype(BF16))


def _pick(m, cands):
    for c in cands:
        if m % c == 0:
            return c
    raise ValueError(f"no tile for {m}")


def _trunk(x, meta_tokens, layers, norms, final_g):
    batch, seq, d = x.shape
    seq_rows = FRONT + seq
    m = batch * seq_rows
    meta = jnp.broadcast_to(meta_tokens.astype(x.dtype)[None], (batch, N_META, d))
    h = jnp.concatenate([jnp.zeros((batch, PADR, d), x.dtype), meta, x], axis=1).reshape(m, d)

    dn_w = DN_H * DN_HD
    df_qw = DF_H * 2 * DF_HD
    tm = _pick(m, (768, 512, 256))
    tr = 256
    tq = _pick(seq_rows, (768, 512, 256))
    slopes = jnp.exp2(-8.0 * (jnp.arange(DF_H, dtype=F32) + 1.0) / DF_H)
    n_main = 4 * dn_w + 2 * df_qw + DF_H * DF_VD
    tn_main = _pick(n_main, (1024, 512, 256))

    for l, (p, nm) in enumerate(zip(layers, norms)):
        lambda_init = 0.8 - 0.6 * math.exp(-0.3 * l)
        u = rmsnorm_rows(h, nm["attn_norm"], tr)
        z = matmul(u, p["w_main"], tm, tn_main, BF16)
        z_ab = matmul(u, p["w_ab"], tm, p["w_ab"].shape[1], F32)
        qkv = gdn_prep(z, nm["dn_conv_w"], seq_rows, tr)
        gates = gdn_gates(z_ab, p["gate_params"], seq_rows, tr)
        o_f, o_b = gdn_scan(qkv, gates, batch, seq_rows)
        o_dn = gdn_post(o_f, o_b, z, 3, nm["dn_norm"], tr)
        qk = qk_norm(z, (4 * dn_w) // df_qw, jnp.stack([nm["df_q_norm"], nm["df_k_norm"]]), tr)
        o_df = diff_attention(qk, z, (4 * dn_w + 2 * df_qw) // DF_VD, slopes, nm["df_lambda"],
                              nm["df_subln"], lambda_init, batch, seq_rows, tq, tq)
        h = outproj_residual(o_dn, o_df, p["w_out"], h, tm, _pick(d, (1024, 512, 256)))
        u = rmsnorm_rows(h, nm["ffn_norm"], tr)
        f = p["w_gate"].shape[1]
        act = ffn_gate_up(u, p["w_gate"], p["w_up"], nm["ffn_conv_w"], nm["ffn_conv_b"], tm,
                          _pick(f, (512, 256, 128)))
        nkb = 2 if (f // 2) % LANES == 0 else 1
        for kb in range(nkb):
            h = matmul_residual(act, p["w_down"], h, tm, _pick(d, (512, 256)), kb, nkb)
    return final_norm_rows(h, final_g, batch, seq)


def kernel(x_prompt, x_sample, meta_tokens, attn_norm, w_in, dn_conv_w, dn_A_log, dn_dt_bias, dn_norm, df_q_norm, df_k_norm, df_lambda, df_subln, w_out, ffn_norm, w_gate, w_up, ffn_conv_w, ffn_conv_b, w_down, final_norm):
    depth = w_in.shape[0]
    layers = [_prep_layer_params(l, w_in, dn_A_log, dn_dt_bias, w_out, w_gate, w_up, w_down)
              for l in range(depth)]
    norms = [dict(attn_norm=attn_norm[l], dn_conv_w=dn_conv_w[l], dn_norm=dn_norm[l],
                  df_q_norm=df_q_norm[l], df_k_norm=df_k_norm[l], df_lambda=df_lambda[l],
                  df_subln=df_subln[l], ffn_norm=ffn_norm[l], ffn_conv_w=ffn_conv_w[l],
                  ffn_conv_b=ffn_conv_b[l]) for l in range(depth)]
    y_prompt = _trunk(x_prompt, meta_tokens, layers, norms, final_norm)
    y_sample = _trunk(x_sample, meta_tokens, layers, norms, final_norm)
    return (y_prompt, y_sample)
```

```python
import functools
import math

import jax
import jax.numpy as jnp
from jax import lax
from jax.experimental import pallas as pl
from jax.experimental.pallas import tpu as pltpu

N_META = 16
DN_H = 16
DN_HD = 128
DF_H = 8
DF_HD = 128
DF_VD = 2 * DF_HD
EPS = 1e-6
LOG2E = math.log2(math.e)

FRONT = 256
PADR = FRONT - N_META
GCH = 128
INV_BASE = 16
HPG = 8
FFN_ROW_SPLIT = 2
LANES = 128
HALO = 16
VMEM_LIMIT = 48 * 1024 * 1024

F32 = jnp.float32
BF16 = jnp.bfloat16


def _cparams(sem):
    return pltpu.CompilerParams(dimension_semantics=sem, vmem_limit_bytes=VMEM_LIMIT)


def _dot(a, b):
    return jnp.dot(a, b, preferred_element_type=F32)


def _dot_nt(a, b):
    return lax.dot_general(a, b, (((1,), (1,)), ((), ())), preferred_element_type=F32)


def _dot_tn(a, b):
    return lax.dot_general(a, b, (((0,), (0,)), ((), ())), preferred_element_type=F32)


def _silu(x):
    return x * (1.0 / (1.0 + jnp.exp(-x)))


def _rmsnorm_kernel(h_ref, g_ref, o_ref):
    x = h_ref[...]
    ms = jnp.mean(x * x, axis=-1, keepdims=True)
    o_ref[...] = (x * lax.rsqrt(ms + EPS) * g_ref[...]).astype(o_ref.dtype)


def rmsnorm_rows(h, g, tr):
    m, d = h.shape
    return pl.pallas_call(
        _rmsnorm_kernel,
        grid=(m // tr,),
        in_specs=[pl.BlockSpec((tr, d), lambda i: (i, 0)),
                  pl.BlockSpec((1, d), lambda i: (0, 0))],
        out_specs=pl.BlockSpec((tr, d), lambda i: (i, 0)),
        out_shape=jax.ShapeDtypeStruct((m, d), BF16),
        compiler_params=_cparams(("parallel",)),
        name="rmsnorm_rows",
    )(h, g.reshape(1, d))


def _final_norm_kernel(h_ref, g_ref, o_ref):
    x = h_ref[...]
    ms = jnp.mean(x * x, axis=-1, keepdims=True)
    o_ref[0] = x * lax.rsqrt(ms + EPS) * g_ref[...]


def final_norm_rows(h, g, batch, seq):
    m, d = h.shape
    tr = LANES
    nt = seq // tr
    nb = (FRONT + seq) // tr
    off = FRONT // tr
    return pl.pallas_call(
        _final_norm_kernel,
        grid=(batch, nt),
        in_specs=[pl.BlockSpec((tr, d), lambda b, i: (b * nb + off + i, 0)),
                  pl.BlockSpec((1, d), lambda b, i: (0, 0))],
        out_specs=pl.BlockSpec((1, tr, d), lambda b, i: (b, i, 0)),
        out_shape=jax.ShapeDtypeStruct((batch, seq, d), F32),
        compiler_params=_cparams(("parallel", "parallel")),
        name="final_norm",
    )(h, g.reshape(1, d))


def _matmul_kernel(x_ref, w_ref, o_ref):
    o_ref[...] = _dot(x_ref[...], w_ref[...]).astype(o_ref.dtype)


def matmul(x, w, tm, tn, out_dtype):
    m, k = x.shape
    n = w.shape[1]
    return pl.pallas_call(
        _matmul_kernel,
        grid=(m // tm, n // tn),
        in_specs=[pl.BlockSpec((tm, k), lambda i, j: (i, 0)),
                  pl.BlockSpec((k, tn), lambda i, j: (0, j))],
        out_specs=pl.BlockSpec((tm, tn), lambda i, j: (i, j)),
        out_shape=jax.ShapeDtypeStruct((m, n), out_dtype),
        compiler_params=_cparams(("parallel", "arbitrary")),
        name="matmul",
    )(x, w)


def _matmul_res_kernel(x_ref, w_ref, h_ref, o_ref):
    o_ref[...] = h_ref[...] + _dot(x_ref[...], w_ref[...])


def matmul_residual(x, w, h, tm, tn, kblk, nkb):
    m = x.shape[0]
    kb = x.shape[1] // nkb
    n = w.shape[1]
    return pl.pallas_call(
        _matmul_res_kernel,
        grid=(m // tm, n // tn),
        in_specs=[pl.BlockSpec((tm, kb), lambda i, j: (i, kblk)),
                  pl.BlockSpec((kb, tn), lambda i, j: (kblk, j)),
                  pl.BlockSpec((tm, tn), lambda i, j: (i, j))],
        out_specs=pl.BlockSpec((tm, tn), lambda i, j: (i, j)),
        out_shape=jax.ShapeDtypeStruct((m, n), F32),
        input_output_aliases={2: 0},
        compiler_params=_cparams(("parallel", "arbitrary")),
        name="matmul_residual",
    )(x, w, h)


def _outproj_kernel(x1_ref, x2_ref, w1_ref, w2_ref, h_ref, o_ref):
    o_ref[...] = h_ref[...] + (_dot(x1_ref[...], w1_ref[...]) + _dot(x2_ref[...], w2_ref[...]))


def outproj_residual(x1, x2, w, h, tm, tn):
    m, k1 = x1.shape
    k2 = x2.shape[1]
    assert k1 == k2 and w.shape[0] == k1 + k2
    n = w.shape[1]
    return pl.pallas_call(
        _outproj_kernel,
        grid=(m // tm, n // tn),
        in_specs=[pl.BlockSpec((tm, k1), lambda i, j: (i, 0)),
                  pl.BlockSpec((tm, k2), lambda i, j: (i, 0)),
                  pl.BlockSpec((k1, tn), lambda i, j: (0, j)),
                  pl.BlockSpec((k2, tn), lambda i, j: (1, j)),
                  pl.BlockSpec((tm, tn), lambda i, j: (i, j))],
        out_specs=pl.BlockSpec((tm, tn), lambda i, j: (i, j)),
        out_shape=jax.ShapeDtypeStruct((m, n), F32),
        input_output_aliases={4: 0},
        compiler_params=_cparams(("parallel", "arbitrary")),
        name="outproj_residual",
    )(x1, x2, w, w, h)


def _shift_rows(x, prev_row, next_row):
    t = x.shape[0]
    row = lax.broadcasted_iota(jnp.int32, x.shape, 0)
    xp = jnp.where(row == 0, prev_row, pltpu.roll(x, 1, axis=0))
    xn = jnp.where(row == t - 1, next_row, pltpu.roll(x, t - 1, axis=0))
    return xp, xn


def _ffn_gate_up_kernel(x_ref, xp_ref, xn_ref, wg_ref, wu_ref, cw_ref, cb_ref, o_ref):
    i = pl.program_id(0)
    ni = pl.num_programs(0)
    wg = wg_ref[...]
    wu = wu_ref[...]
    cw = cw_ref[...]
    cb = cb_ref[...]
    tm = x_ref.shape[0]
    ts = tm // FFN_ROW_SPLIT
    g = [_dot(x_ref[r * ts:(r + 1) * ts, :], wg) for r in range(FFN_ROW_SPLIT)]
    gp = _dot(xp_ref[...], wg)[HALO - 1:HALO, :] * jnp.where(i > 0, 1.0, 0.0)
    gn = _dot(xn_ref[...], wg)[0:1, :] * jnp.where(i < ni - 1, 1.0, 0.0)
    for r in range(FFN_ROW_SPLIT):
        prev_row = gp if r == 0 else g[r - 1][ts - 1:ts, :]
        next_row = gn if r == FFN_ROW_SPLIT - 1 else g[r + 1][0:1, :]
        g_prev, g_next = _shift_rows(g[r], prev_row, next_row)
        gt = g_prev * cw[0:1, :] + g[r] * cw[1:2, :] + g_next * cw[2:3, :] + cb
        up = _dot(x_ref[r * ts:(r + 1) * ts, :], wu)
        o_ref[r * ts:(r + 1) * ts, :] = (_silu(gt) * up).astype(o_ref.dtype)


def ffn_gate_up(x, wg, wu, cw, cb, tm, tf):
    m, k = x.shape
    f = wg.shape[1]
    hb = tm // HALO
    nhb = m // HALO
    return pl.pallas_call(
        _ffn_gate_up_kernel,
        grid=(m // tm, f // tf),
        in_specs=[pl.BlockSpec((tm, k), lambda i, j: (i, 0)),
                  pl.BlockSpec((HALO, k), lambda i, j: (jnp.maximum(i * hb - 1, 0), 0)),
                  pl.BlockSpec((HALO, k), lambda i, j: (jnp.minimum((i + 1) * hb, nhb - 1), 0)),
                  pl.BlockSpec((k, tf), lambda i, j: (0, j)),
                  pl.BlockSpec((k, tf), lambda i, j: (0, j)),
                  pl.BlockSpec((3, tf), lambda i, j: (0, j)),
                  pl.BlockSpec((1, tf), lambda i, j: (0, j))],
        out_specs=pl.BlockSpec((tm, tf), lambda i, j: (i, j)),
        out_shape=jax.ShapeDtypeStruct((m, f), BF16),
        compiler_params=_cparams(("parallel", "arbitrary")),
        name="ffn_gate_up",
    )(x, x, x, wg, wu, cw, cb.reshape(1, f))


def _valid_rows(tile_idx, tiles_per_seq, tr):
    local = lax.rem(tile_idx, tiles_per_seq) * tr
    row = local + lax.broadcasted_iota(jnp.int32, (tr, 1), 0)
    return row >= PADR


def _gdn_prep_kernel(x_ref, xp_ref, xn_ref, cw_ref, o_ref, *, tiles_per_seq, n_heads):
    i = pl.program_id(0)
    j = pl.program_id(1)
    ni = pl.num_programs(0)
    x = x_ref[...].astype(F32)
    tr = x.shape[0]
    prev_row = xp_ref[HALO - 1:HALO, :].astype(F32) * jnp.where(i > 0, 1.0, 0.0)
    next_row = xn_ref[0:1, :].astype(F32) * jnp.where(i < ni - 1, 1.0, 0.0)
    x_prev, x_next = _shift_rows(x, prev_row, next_row)
    cw = cw_ref[...]
    y = _silu(x_prev * cw[0:1, :] + x * cw[1:2, :] + x_next * cw[2:3, :])
    valid = _valid_rows(i, tiles_per_seq, tr)
    is_qk = j < 2
    q_scale = jnp.where(j == 0, DN_HD ** -0.5, 1.0)
    for hh in range(n_heads):
        sl = slice(hh * DN_HD, (hh + 1) * DN_HD)
        yh = y[:, sl]
        nrm = yh * lax.rsqrt(jnp.sum(yh * yh, axis=-1, keepdims=True) + EPS) * q_scale
        yh = jnp.where(is_qk, nrm, yh)
        o_ref[:, sl] = jnp.where(valid, yh, 0.0).astype(o_ref.dtype)


def gdn_prep(z, conv_w, seq_rows, tr):
    m = z.shape[0]
    w = DN_H * DN_HD
    hb = tr // HALO
    nhb = m // HALO
    kern = functools.partial(_gdn_prep_kernel, tiles_per_seq=seq_rows // tr, n_heads=DN_H)
    return pl.pallas_call(
        kern,
        grid=(m // tr, 3),
        in_specs=[pl.BlockSpec((tr, w), lambda i, j: (i, j)),
                  pl.BlockSpec((HALO, w), lambda i, j: (jnp.maximum(i * hb - 1, 0), j)),
                  pl.BlockSpec((HALO, w), lambda i, j: (jnp.minimum((i + 1) * hb, nhb - 1), j)),
                  pl.BlockSpec((3, w), lambda i, j: (0, j))],
        out_specs=pl.BlockSpec((tr, w), lambda i, j: (i, j)),
        out_shape=jax.ShapeDtypeStruct((m, 3 * w), BF16),
        compiler_params=_cparams(("parallel", "arbitrary")),
        name="gdn_prep",
    )(z, z, z, conv_w)


def _gdn_gates_kernel(z_ref, p_ref, o_ref, *, tiles_per_seq):
    i = pl.program_id(0)
    z = z_ref[...]
    tr = z.shape[0]
    a_neg_exp = p_ref[0:1, :]
    dt_bias = p_ref[1:2, :]
    kind = p_ref[2:3, :]
    x = z + dt_bias
    softplus = jnp.maximum(x, 0.0) + jnp.log(1.0 + jnp.exp(-jnp.abs(x)))
    g = a_neg_exp * softplus
    beta = 1.0 / (1.0 + jnp.exp(-z))
    out = jnp.where(kind == 1.0, g, jnp.where(kind == 2.0, beta, 0.0))
    valid = _valid_rows(i, tiles_per_seq, tr)
    o_ref[...] = jnp.where(valid, out, 0.0)


def gdn_gates(z_ab, params, seq_rows, tr):
    m, n = z_ab.shape
    kern = functools.partial(_gdn_gates_kernel, tiles_per_seq=seq_rows // tr)
    return pl.pallas_call(
        kern,
        grid=(m // tr,),
        in_specs=[pl.BlockSpec((tr, n), lambda i: (i, 0)),
                  pl.BlockSpec((3, n), lambda i: (0, 0))],
        out_specs=pl.BlockSpec((tr, n), lambda i: (i, 0)),
        out_shape=jax.ShapeDtypeStruct((m, n), F32),
        compiler_params=_cparams(("parallel",)),
        name="gdn_gates",
    )(z_ab, params)


def _unit_triangular_inverses(a_mats, row, col):
    c = a_mats[0].shape[0]
    eye = (row == col).astype(F32)
    same = (row // INV_BASE) == (col // INV_BASE)
    diag = [jnp.where(same, a, 0.0) for a in a_mats]
    t_mats = [eye - b for b in diag]
    n_lvl = int(math.log2(INV_BASE)) - 1
    if n_lvl > 0:
        b_bf = [b.astype(BF16) for b in diag]
        p_mats = [_dot(b, b) for b in b_bf]
    for lvl in range(n_lvl):
        p_bf = [p.astype(BF16) for p in p_mats]
        if lvl < n_lvl - 1:
            xs = [_dot(jnp.concatenate([t.astype(BF16), p], axis=0), p) for t, p in zip(t_mats, p_bf)]
            t_mats = [t + x[:c] for t, x in zip(t_mats, xs)]
            p_mats = [x[c:] for x in xs]
        else:
            t_mats = [t + _dot(t.astype(BF16), p) for t, p in zip(t_mats, p_bf)]
    size = INV_BASE
    while size < c:
        off = ((row // (2 * size)) == (col // (2 * size))) & ((row // size) != (col // size))
        t_bf = [t.astype(BF16) for t in t_mats]
        ys = [_dot(jnp.where(off, a, 0.0).astype(BF16), t) for a, t in zip(a_mats, t_bf)]
        t_mats = [t - _dot(tb, y.astype(BF16)) for t, tb, y in zip(t_mats, t_bf, ys)]
        size *= 2
    return t_mats


def _gdn_scan_kernel(qf_ref, kf_ref, vf_ref, gf_ref, qb_ref, kb_ref, vb_ref, gb_ref,
                     of_ref, ob_ref, s_ref):
    @pl.when(pl.program_id(2) == 0)
    def _():
        s_ref[...] = jnp.zeros_like(s_ref)

    c = GCH
    d = DN_HD
    row = lax.broadcasted_iota(jnp.int32, (c, c), 0)
    col = lax.broadcasted_iota(jnp.int32, (c, c), 1)
    q, k, v, g_col, g_row, g_last, beta, incl, strict, outs = ([] for _ in range(10))
    for direction, (q_ref, k_ref, v_ref, g_ref, o_ref) in enumerate(
            ((qf_ref, kf_ref, vf_ref, gf_ref, of_ref), (qb_ref, kb_ref, vb_ref, gb_ref, ob_ref))):
        reverse = direction == 1
        gates = g_ref[...]
        mask_incl = (row <= col) if reverse else (row >= col)
        cum = jnp.dot(mask_incl.astype(F32), gates, preferred_element_type=F32,
                      precision=lax.Precision.HIGHEST)
        cum_t = cum.T
        for hh in range(HPG):
            sl = slice(hh * d, (hh + 1) * d)
            lane_g = direction * HPG + hh
            lane_b = 2 * HPG + lane_g
            q.append(q_ref[:, sl])
            k.append(k_ref[:, sl])
            v.append(v_ref[:, sl])
            gc = cum[:, lane_g:lane_g + 1]
            g_col.append(gc)
            g_row.append(cum_t[lane_g:lane_g + 1, :])
            g_last.append(gc[0:1, :] if reverse else gc[c - 1:c, :])
            beta.append(gates[:, lane_b:lane_b + 1])
            incl.append(mask_incl)
            strict.append((row < col) if reverse else (row > col))
            outs.append((o_ref, sl))
    n = len(q)
    rng = range(n)
    decay = [jnp.where(incl[i], jnp.exp(jnp.where(incl[i], g_col[i] - g_row[i], 0.0)), 0.0) for i in rng]
    kf = [k[i].astype(F32) for i in rng]
    kbeta = [kf[i] * beta[i] for i in rng]
    e_g = [jnp.exp(g_col[i]) for i in rng]
    kk = [_dot_nt(jnp.concatenate([kbeta[i].astype(BF16), q[i]], axis=0), k[i]) for i in rng]
    a_mats = [jnp.where(strict[i], kk[i][:c] * decay[i], 0.0) for i in rng]
    attn = [(kk[i][c:] * decay[i]).astype(BF16) for i in rng]
    t_mats = _unit_triangular_inverses(a_mats, row, col)
    rhs = [jnp.concatenate([(v[i].astype(F32) * beta[i]).astype(BF16),
                            (kbeta[i] * e_g[i]).astype(BF16)], axis=1) for i in rng]
    uw = [_dot(t_mats[i].astype(BF16), rhs[i]) for i in rng]
    s = [s_ref[i] for i in rng]
    qs = [_dot(jnp.concatenate([(q[i].astype(F32) * e_g[i]).astype(BF16), uw[i][:, d:].astype(BF16)], axis=0),
               s[i].astype(BF16)) for i in rng]
    v_new = [(uw[i][:, :d] - qs[i][c:]).astype(BF16) for i in rng]
    o = [qs[i][:c] + _dot(attn[i], v_new[i]) for i in rng]
    k_dec = [(kf[i] * jnp.exp(g_last[i] - g_col[i])).astype(BF16) for i in rng]
    s_new = [s[i] * jnp.exp(g_last[i]) + _dot_tn(k_dec[i], v_new[i]) for i in rng]
    for i in rng:
        s_ref[i] = s_new[i]
        o_ref, sl = outs[i]
        o_ref[:, sl] = o[i].astype(o_ref.dtype)


def gdn_scan(qkv, gates, batch, seq_rows):
    m = qkv.shape[0]
    w = DN_H * DN_HD
    ng = DN_H // HPG
    gw = HPG * DN_HD
    nb = seq_rows // GCH

    def fwd(colblk):
        return lambda b, g, s: (b * nb + s, colblk * ng + g)

    def bwd(colblk):
        return lambda b, g, s: (b * nb + nb - 1 - s, colblk * ng + g)

    blk = (GCH, gw)
    return pl.pallas_call(
        _gdn_scan_kernel,
        grid=(batch, ng, nb),
        in_specs=[pl.BlockSpec(blk, fwd(0)), pl.BlockSpec(blk, fwd(1)), pl.BlockSpec(blk, fwd(2)),
                  pl.BlockSpec((GCH, LANES), lambda b, g, s: (b * nb + s, g)),
                  pl.BlockSpec(blk, bwd(0)), pl.BlockSpec(blk, bwd(1)), pl.BlockSpec(blk, bwd(2)),
                  pl.BlockSpec((GCH, LANES), lambda b, g, s: (b * nb + nb - 1 - s, g))],
        out_specs=[pl.BlockSpec(blk, fwd(0)), pl.BlockSpec(blk, bwd(0))],
        out_shape=[jax.ShapeDtypeStruct((m, w), BF16), jax.ShapeDtypeStruct((m, w), BF16)],
        scratch_shapes=[pltpu.VMEM((2 * HPG, DN_HD, DN_HD), F32)],
        compiler_params=_cparams(("parallel", "parallel", "arbitrary")),
        name="gdn_scan",
    )(qkv, qkv, qkv, gates, qkv, qkv, qkv, gates)


def _gdn_post_kernel(of_ref, ob_ref, gate_ref, g_ref, o_ref, *, n_heads):
    gain = g_ref[...]
    for hh in range(n_heads):
        sl = slice(hh * DN_HD, (hh + 1) * DN_HD)
        o = of_ref[:, sl].astype(F32) + ob_ref[:, sl].astype(F32)
        y = o * lax.rsqrt(jnp.mean(o * o, axis=-1, keepdims=True) + EPS) * gain
        o_ref[:, sl] = (y * _silu(gate_ref[:, sl].astype(F32))).astype(o_ref.dtype)


def gdn_post(o_f, o_b, z, gate_colblk, gain, tr):
    m, w = o_f.shape
    kern = functools.partial(_gdn_post_kernel, n_heads=DN_H)
    return pl.pallas_call(
        kern,
        grid=(m // tr,),
        in_specs=[pl.BlockSpec((tr, w), lambda i: (i, 0)),
                  pl.BlockSpec((tr, w), lambda i: (i, 0)),
                  pl.BlockSpec((tr, w), lambda i: (i, gate_colblk)),
                  pl.BlockSpec((1, DN_HD), lambda i: (0, 0))],
        out_specs=pl.BlockSpec((tr, w), lambda i: (i, 0)),
        out_shape=jax.ShapeDtypeStruct((m, w), BF16),
        compiler_params=_cparams(("parallel",)),
        name="gdn_post",
    )(o_f, o_b, z, gain.reshape(1, DN_HD))


def _qk_norm_kernel(x_ref, g_ref, o_ref, *, n_groups):
    j = pl.program_id(1)
    gain = g_ref[0]
    scale = jnp.where(j == 0, DF_HD ** -0.5 * LOG2E, 1.0)
    for gi in range(n_groups):
        sl = slice(gi * DF_HD, (gi + 1) * DF_HD)
        x = x_ref[:, sl].astype(F32)
        y = x * lax.rsqrt(jnp.mean(x * x, axis=-1, keepdims=True) + EPS) * gain
        o_ref[:, sl] = (y * scale).astype(o_ref.dtype)


def qk_norm(z, q_colblk, gains, tr):
    m = z.shape[0]
    w = DF_H * 2 * DF_HD
    kern = functools.partial(_qk_norm_kernel, n_groups=DF_H * 2)
    return pl.pallas_call(
        kern,
        grid=(m // tr, 2),
        in_specs=[pl.BlockSpec((tr, w), lambda i, j: (i, q_colblk + j)),
                  pl.BlockSpec((1, 1, DF_HD), lambda i, j: (j, 0, 0))],
        out_specs=pl.BlockSpec((tr, w), lambda i, j: (i, j)),
        out_shape=jax.ShapeDtypeStruct((m, 2 * w), BF16),
        compiler_params=_cparams(("parallel", "arbitrary")),
        name="qk_norm",
    )(z, gains.reshape(2, 1, DF_HD))


def _attn_kernel(slopes_ref, q_ref, k_ref, v_ref, lp_ref, sg_ref, o_ref,
                 acc_ref, m_ref, l_ref, *, tk, lambda_init):
    h = pl.program_id(1)
    qi = pl.program_id(2)
    tq = q_ref.shape[0]
    nk = k_ref.shape[0] // tk
    slope2 = slopes_ref[h] * LOG2E
    q = q_ref[...]
    q_maps = (q[:, :DF_HD], q[:, DF_HD:])
    q_row = (qi * tq + lax.broadcasted_iota(jnp.int32, (tq, 1), 0)).astype(F32)
    key_off = lax.broadcasted_iota(jnp.int32, (1, tk), 1).astype(F32)
    acc_ref[...] = jnp.zeros_like(acc_ref)
    m_ref[...] = jnp.full_like(m_ref, -1e30)
    l_ref[...] = jnp.zeros_like(l_ref)

    def chunk(c, mask_pad):
        k0 = pl.multiple_of(c * tk, tk)
        kc = k_ref[pl.ds(k0, tk), :]
        vc = v_ref[pl.ds(k0, tk), :]
        k0f = k0.astype(F32)
        mixed = c == (qi * tq) // tk

        def run(key_term, row_term):
            for mp in range(2):
                x = _dot_nt(q_maps[mp], kc[:, mp * DF_HD:(mp + 1) * DF_HD]) + key_term
                if mask_pad:
                    x = jnp.where(key_off >= float(PADR), x, -1e30)
                m_old = m_ref[mp]
                m_loc = jnp.max(x, axis=-1, keepdims=True)
                if row_term is not None:
                    m_loc = m_loc + row_term
                m_new = jnp.maximum(m_old, m_loc)
                alpha = jnp.exp2(m_old - m_new)
                p = jnp.exp2(x - (m_new if row_term is None else m_new - row_term))
                l_ref[mp] = alpha * l_ref[mp] + jnp.sum(p, axis=-1, keepdims=True)
                m_ref[mp] = m_new
                acc_ref[mp] = alpha * acc_ref[mp] + _dot(p.astype(BF16), vc)

        @pl.when(mixed)
        def _():
            run(-slope2 * jnp.abs(q_row - (k0f + key_off)), None)

        @pl.when(jnp.logical_not(mixed))
        def _():
            sgn = jnp.where(k0 < qi * tq, slope2, -slope2)
            run(sgn * key_off, -sgn * (q_row - k0f))

    chunk(jnp.int32(0), True)

    def body(c, carry):
        chunk(c, False)
        return carry

    lax.fori_loop(1, nk, body, 0)

    lp = lp_ref[...]
    lam = (jnp.exp(jnp.sum(lp[0:1] * lp[1:2], axis=-1, keepdims=True))
           - jnp.exp(jnp.sum(lp[2:3] * lp[3:4], axis=-1, keepdims=True)) + lambda_init)
    o = acc_ref[0] / l_ref[0] - lam * (acc_ref[1] / l_ref[1])
    y = o * lax.rsqrt(jnp.mean(o * o, axis=-1, keepdims=True) + EPS) * sg_ref[...]
    y = y * (1.0 - lambda_init)
    o_ref[...] = jnp.where(q_row >= float(PADR), y, 0.0).astype(o_ref.dtype)


def diff_attention(qk, z, v_colblk, slopes, lam_p, subln_g, lambda_init, batch, seq_rows, tq, tk):
    m = qk.shape[0]
    nq = seq_rows // tq
    hw = 2 * DF_HD
    assert tk % tq == 0 and seq_rows % tk == 0 and tk > PADR
    kern = functools.partial(_attn_kernel, tk=tk, lambda_init=lambda_init)
    return pl.pallas_call(
        kern,
        grid_spec=pltpu.PrefetchScalarGridSpec(
            num_scalar_prefetch=1,
            grid=(batch, DF_H, nq),
            in_specs=[pl.BlockSpec((tq, hw), lambda b, h, i, s: (b * nq + i, h)),
                      pl.BlockSpec((seq_rows, hw), lambda b, h, i, s: (b, DF_H + h)),
                      pl.BlockSpec((seq_rows, DF_VD), lambda b, h, i, s: (b, v_colblk + h)),
                      pl.BlockSpec((4, DF_HD), lambda b, h, i, s: (0, 0)),
                      pl.BlockSpec((1, DF_VD), lambda b, h, i, s: (0, 0))],
            out_specs=pl.BlockSpec((tq, DF_VD), lambda b, h, i, s: (b * nq + i, h)),
            scratch_shapes=[pltpu.VMEM((2, tq, DF_VD), F32),
                            pltpu.VMEM((2, tq, 1), F32),
                            pltpu.VMEM((2, tq, 1), F32)]),
        out_shape=jax.ShapeDtypeStruct((m, DF_H * DF_VD), BF16),
        compiler_params=_cparams(("parallel", "parallel", "arbitrary")),
        name="diff_attention",
    )(slopes, qk, qk, z, lam_p, subln_g.reshape(1, DF_VD))


def _prep_layer_params(l, w_in, dn_A_log, dn_dt_bias, w_out, w_gate, w_up, w_down):
    dn_w = DN_H * DN_HD
    df_w = DF_H * DF_VD
    n_ab = 2 * DN_H
    ab0 = 4 * dn_w
    w = w_in[l]
    w_main = jnp.concatenate([w[:, :ab0], w[:, ab0 + 2 * n_ab:]], axis=1).astype(BF16)
    wa = w[:, ab0:ab0 + n_ab].reshape(-1, 2, DN_H // HPG, HPG)
    wb = w[:, ab0 + n_ab:ab0 + 2 * n_ab].reshape(-1, 2, DN_H // HPG, HPG)
    lanes = jnp.concatenate([wa.transpose(0, 2, 1, 3), wb.transpose(0, 2, 1, 3)], axis=2)
    lanes = lanes.reshape(w.shape[0], DN_H // HPG, 4 * HPG)
    w_ab = jnp.pad(lanes, ((0, 0), (0, 0), (0, LANES - 4 * HPG))).reshape(w.shape[0], -1).astype(BF16)

    def gate_lanes(p):
        p = p.reshape(2, DN_H // HPG, HPG).transpose(1, 0, 2).reshape(DN_H // HPG, 2 * HPG)
        return jnp.pad(p, ((0, 0), (0, LANES - 2 * HPG))).reshape(1, -1)

    kind = jnp.concatenate([jnp.full((2 * HPG,), 1.0, F32), jnp.full((2 * HPG,), 2.0, F32),
                            jnp.zeros((LANES - 4 * HPG,), F32)])
    kind = jnp.tile(kind, DN_H // HPG).reshape(1, -1)
    gate_params = jnp.concatenate([gate_lanes(-jnp.exp(dn_A_log[l].astype(F32))),
                                   gate_lanes(dn_dt_bias[l].astype(F32)), kind], axis=0)
    return dict(w_main=w_main, w_ab=w_ab, gate_params=gate_params,
                w_out=w_out[l].astype(BF16), w_gate=w_gate[l].astype(BF16),
                w_up=w_up[l].astype(BF16), w_down=w_down[l].astype(BF16))


def _pick(m, cands):
    for c in cands:
        if m % c == 0:
            return c
    raise ValueError(f"no tile for {m}")


def _trunk(x, meta_tokens, layers, norms, final_g):
    batch, seq, d = x.shape
    seq_rows = FRONT + seq
    m = batch * seq_rows
    meta = jnp.broadcast_to(meta_tokens.astype(x.dtype)[None], (batch, N_META, d))
    h = jnp.concatenate([jnp.zeros((batch, PADR, d), x.dtype), meta, x], axis=1).reshape(m, d)

    dn_w = DN_H * DN_HD
    df_qw = DF_H * 2 * DF_HD
    tm = _pick(m, (768, 512, 256))
    tr = 256
    tq = _pick(seq_rows, (768, 512, 256))
    slopes = jnp.exp2(-8.0 * (jnp.arange(DF_H, dtype=F32) + 1.0) / DF_H)
    n_main = 4 * dn_w + 2 * df_qw + DF_H * DF_VD
    tn_main = _pick(n_main, (1024, 512, 256))

    for l, (p, nm) in enumerate(zip(layers, norms)):
        lambda_init = 0.8 - 0.6 * math.exp(-0.3 * l)
        u = rmsnorm_rows(h, nm["attn_norm"], tr)
        z = matmul(u, p["w_main"], tm, tn_main, BF16)
        z_ab = matmul(u, p["w_ab"], tm, p["w_ab"].shape[1], F32)
        qkv = gdn_prep(z, nm["dn_conv_w"], seq_rows, tr)
        gates = gdn_gates(z_ab, p["gate_params"], seq_rows, tr)
        o_f, o_b = gdn_scan(qkv, gates, batch, seq_rows)
        o_dn = gdn_post(o_f, o_b, z, 3, nm["dn_norm"], tr)
        qk = qk_norm(z, (4 * dn_w) // df_qw, jnp.stack([nm["df_q_norm"], nm["df_k_norm"]]), tr)
        o_df = diff_attention(qk, z, (4 * dn_w + 2 * df_qw) // DF_VD, slopes, nm["df_lambda"],
                              nm["df_subln"], lambda_init, batch, seq_rows, tq, tq)
        h = outproj_residual(o_dn, o_df, p["w_out"], h, tm, _pick(d, (1024, 512, 256)))
        u = rmsnorm_rows(h, nm["ffn_norm"], tr)
        f = p["w_gate"].shape[1]
        act = ffn_gate_up(u, p["w_gate"], p["w_up"], nm["ffn_conv_w"], nm["ffn_conv_b"],
                          _pick(m, (FFN_ROW_SPLIT * tm, tm)), _pick(f, (256, 128)))
        nkb = 2 if (f // 2) % LANES == 0 else 1
        for kb in range(nkb):
            h = matmul_residual(act, p["w_down"], h, tm, _pick(d, (512, 256)), kb, nkb)
    return final_norm_rows(h, final_g, batch, seq)


def kernel(x_prompt, x_sample, meta_tokens, attn_norm, w_in, dn_conv_w, dn_A_log, dn_dt_bias, dn_norm, df_q_norm, df_k_norm, df_lambda, df_subln, w_out, ffn_norm, w_gate, w_up, ffn_conv_w, ffn_conv_b, w_down, final_norm):
    depth = w_in.shape[0]
    layers = [_prep_layer_params(l, w_in, dn_A_log, dn_dt_bias, w_out, w_gate, w_up, w_down)
              for l in range(depth)]
    norms = [dict(attn_norm=attn_norm[l], dn_conv_w=dn_conv_w[l], dn_norm=dn_norm[l],
                  df_q_norm=df_q_norm[l], df_k_norm=df_k_norm[l], df_lambda=df_lambda[l],
                  df_subln=df_subln[l], ffn_norm=ffn_norm[l], ffn_conv_w=ffn_conv_w[l],
                  ffn_conv_b=ffn_conv_b[l]) for l in range(depth)]
    y_prompt = _trunk(x_prompt, meta_tokens, layers, norms, final_norm)
    y_sample = _trunk(x_sample, meta_tokens, layers, norms, final_norm)
    return (y_prompt, y_sample)
```

```python
import functools
import math

import jax
import jax.numpy as jnp
from jax import lax
from jax.experimental import pallas as pl
from jax.experimental.pallas import tpu as pltpu

N_META = 16
DN_H = 16
DN_HD = 128
DF_H = 8
DF_HD = 128
DF_VD = 2 * DF_HD
EPS = 1e-6
LOG2E = math.log2(math.e)

FRONTS = (128, 256)
ATTN_TILE = 768
ATTN_ONE_CHUNK = 4096
GCH = 128
INV_BASE = 16
HPG = 8
FFN_ROW_SPLIT = 2
LANES = 128
HALO = 16
VMEM_LIMIT = 48 * 1024 * 1024

F32 = jnp.float32
BF16 = jnp.bfloat16


def _cparams(sem):
    return pltpu.CompilerParams(dimension_semantics=sem, vmem_limit_bytes=VMEM_LIMIT)


def _dot(a, b):
    return jnp.dot(a, b, preferred_element_type=F32)


def _dot_nt(a, b):
    return lax.dot_general(a, b, (((1,), (1,)), ((), ())), preferred_element_type=F32)


def _dot_tn(a, b):
    return lax.dot_general(a, b, (((0,), (0,)), ((), ())), preferred_element_type=F32)


def _silu(x):
    return x * (1.0 / (1.0 + jnp.exp(-x)))


def _rmsnorm_kernel(h_ref, g_ref, o_ref):
    x = h_ref[...]
    ms = jnp.mean(x * x, axis=-1, keepdims=True)
    o_ref[...] = (x * lax.rsqrt(ms + EPS) * g_ref[...]).astype(o_ref.dtype)


def rmsnorm_rows(h, g, tr):
    m, d = h.shape
    return pl.pallas_call(
        _rmsnorm_kernel,
        grid=(m // tr,),
        in_specs=[pl.BlockSpec((tr, d), lambda i: (i, 0)),
                  pl.BlockSpec((1, d), lambda i: (0, 0))],
        out_specs=pl.BlockSpec((tr, d), lambda i: (i, 0)),
        out_shape=jax.ShapeDtypeStruct((m, d), BF16),
        compiler_params=_cparams(("parallel",)),
        name="rmsnorm_rows",
    )(h, g.reshape(1, d))


def _final_norm_kernel(h_ref, g_ref, o_ref):
    x = h_ref[...]
    ms = jnp.mean(x * x, axis=-1, keepdims=True)
    o_ref[0] = x * lax.rsqrt(ms + EPS) * g_ref[...]


def final_norm_rows(h, g, batch, seq, front):
    m, d = h.shape
    tr = LANES
    nt = seq // tr
    nb = (front + seq) // tr
    off = front // tr
    return pl.pallas_call(
        _final_norm_kernel,
        grid=(batch, nt),
        in_specs=[pl.BlockSpec((tr, d), lambda b, i: (b * nb + off + i, 0)),
                  pl.BlockSpec((1, d), lambda b, i: (0, 0))],
        out_specs=pl.BlockSpec((1, tr, d), lambda b, i: (b, i, 0)),
        out_shape=jax.ShapeDtypeStruct((batch, seq, d), F32),
        compiler_params=_cparams(("parallel", "parallel")),
        name="final_norm",
    )(h, g.reshape(1, d))


def _matmul_kernel(x_ref, w_ref, o_ref):
    o_ref[...] = _dot(x_ref[...], w_ref[...]).astype(o_ref.dtype)


def matmul(x, w, tm, tn, out_dtype):
    m, k = x.shape
    n = w.shape[1]
    return pl.pallas_call(
        _matmul_kernel,
        grid=(m // tm, n // tn),
        in_specs=[pl.BlockSpec((tm, k), lambda i, j: (i, 0)),
                  pl.BlockSpec((k, tn), lambda i, j: (0, j))],
        out_specs=pl.BlockSpec((tm, tn), lambda i, j: (i, j)),
        out_shape=jax.ShapeDtypeStruct((m, n), out_dtype),
        compiler_params=_cparams(("parallel", "arbitrary")),
        name="matmul",
    )(x, w)


def _matmul_res_kernel(x_ref, w_ref, h_ref, o_ref):
    o_ref[...] = h_ref[...] + _dot(x_ref[...], w_ref[...])


def matmul_residual(x, w, h, tm, tn, kblk, nkb):
    m = x.shape[0]
    kb = x.shape[1] // nkb
    n = w.shape[1]
    return pl.pallas_call(
        _matmul_res_kernel,
        grid=(m // tm, n // tn),
        in_specs=[pl.BlockSpec((tm, kb), lambda i, j: (i, kblk)),
                  pl.BlockSpec((kb, tn), lambda i, j: (kblk, j)),
                  pl.BlockSpec((tm, tn), lambda i, j: (i, j))],
        out_specs=pl.BlockSpec((tm, tn), lambda i, j: (i, j)),
        out_shape=jax.ShapeDtypeStruct((m, n), F32),
        input_output_aliases={2: 0},
        compiler_params=_cparams(("parallel", "arbitrary")),
        name="matmul_residual",
    )(x, w, h)


def _outproj_kernel(x1_ref, x2_ref, w1_ref, w2_ref, h_ref, o_ref):
    o_ref[...] = h_ref[...] + (_dot(x1_ref[...], w1_ref[...]) + _dot(x2_ref[...], w2_ref[...]))


def outproj_residual(x1, x2, w, h, tm, tn):
    m, k1 = x1.shape
    k2 = x2.shape[1]
    assert k1 == k2 and w.shape[0] == k1 + k2
    n = w.shape[1]
    return pl.pallas_call(
        _outproj_kernel,
        grid=(m // tm, n // tn),
        in_specs=[pl.BlockSpec((tm, k1), lambda i, j: (i, 0)),
                  pl.BlockSpec((tm, k2), lambda i, j: (i, 0)),
                  pl.BlockSpec((k1, tn), lambda i, j: (0, j)),
                  pl.BlockSpec((k2, tn), lambda i, j: (1, j)),
                  pl.BlockSpec((tm, tn), lambda i, j: (i, j))],
        out_specs=pl.BlockSpec((tm, tn), lambda i, j: (i, j)),
        out_shape=jax.ShapeDtypeStruct((m, n), F32),
        input_output_aliases={4: 0},
        compiler_params=_cparams(("parallel", "arbitrary")),
        name="outproj_residual",
    )(x1, x2, w, w, h)


def _shift_rows(x, prev_row, next_row):
    t = x.shape[0]
    row = lax.broadcasted_iota(jnp.int32, x.shape, 0)
    xp = jnp.where(row == 0, prev_row, pltpu.roll(x, 1, axis=0))
    xn = jnp.where(row == t - 1, next_row, pltpu.roll(x, t - 1, axis=0))
    return xp, xn


def _ffn_gate_up_kernel(x_ref, xp_ref, xn_ref, wg_ref, wu_ref, cw_ref, cb_ref, o_ref):
    i = pl.program_id(0)
    ni = pl.num_programs(0)
    wg = wg_ref[...]
    wu = wu_ref[...]
    cw = cw_ref[...]
    cb = cb_ref[...]
    tm = x_ref.shape[0]
    ts = tm // FFN_ROW_SPLIT
    g = [_dot(x_ref[r * ts:(r + 1) * ts, :], wg) for r in range(FFN_ROW_SPLIT)]
    gp = _dot(xp_ref[...], wg)[HALO - 1:HALO, :] * jnp.where(i > 0, 1.0, 0.0)
    gn = _dot(xn_ref[...], wg)[0:1, :] * jnp.where(i < ni - 1, 1.0, 0.0)
    for r in range(FFN_ROW_SPLIT):
        prev_row = gp if r == 0 else g[r - 1][ts - 1:ts, :]
        next_row = gn if r == FFN_ROW_SPLIT - 1 else g[r + 1][0:1, :]
        g_prev, g_next = _shift_rows(g[r], prev_row, next_row)
        gt = g_prev * cw[0:1, :] + g[r] * cw[1:2, :] + g_next * cw[2:3, :] + cb
        up = _dot(x_ref[r * ts:(r + 1) * ts, :], wu)
        o_ref[r * ts:(r + 1) * ts, :] = (_silu(gt) * up).astype(o_ref.dtype)


def ffn_gate_up(x, wg, wu, cw, cb, tm, tf):
    m, k = x.shape
    f = wg.shape[1]
    hb = tm // HALO
    nhb = m // HALO
    return pl.pallas_call(
        _ffn_gate_up_kernel,
        grid=(m // tm, f // tf),
        in_specs=[pl.BlockSpec((tm, k), lambda i, j: (i, 0)),
                  pl.BlockSpec((HALO, k), lambda i, j: (jnp.maximum(i * hb - 1, 0), 0)),
                  pl.BlockSpec((HALO, k), lambda i, j: (jnp.minimum((i + 1) * hb, nhb - 1), 0)),
                  pl.BlockSpec((k, tf), lambda i, j: (0, j)),
                  pl.BlockSpec((k, tf), lambda i, j: (0, j)),
                  pl.BlockSpec((3, tf), lambda i, j: (0, j)),
                  pl.BlockSpec((1, tf), lambda i, j: (0, j))],
        out_specs=pl.BlockSpec((tm, tf), lambda i, j: (i, j)),
        out_shape=jax.ShapeDtypeStruct((m, f), BF16),
        compiler_params=_cparams(("parallel", "arbitrary")),
        name="ffn_gate_up",
    )(x, x, x, wg, wu, cw, cb.reshape(1, f))


def _valid_rows(tile_idx, tiles_per_seq, tr, pad):
    local = lax.rem(tile_idx, tiles_per_seq) * tr
    row = local + lax.broadcasted_iota(jnp.int32, (tr, 1), 0)
    return row >= pad


def _gdn_prep_kernel(x_ref, xp_ref, xn_ref, cw_ref, o_ref, *, tiles_per_seq, n_heads, pad):
    i = pl.program_id(0)
    j = pl.program_id(1)
    ni = pl.num_programs(0)
    x = x_ref[...].astype(F32)
    tr = x.shape[0]
    prev_row = xp_ref[HALO - 1:HALO, :].astype(F32) * jnp.where(i > 0, 1.0, 0.0)
    next_row = xn_ref[0:1, :].astype(F32) * jnp.where(i < ni - 1, 1.0, 0.0)
    x_prev, x_next = _shift_rows(x, prev_row, next_row)
    cw = cw_ref[...]
    y = _silu(x_prev * cw[0:1, :] + x * cw[1:2, :] + x_next * cw[2:3, :])
    valid = _valid_rows(i, tiles_per_seq, tr, pad)
    is_qk = j < 2
    q_scale = jnp.where(j == 0, DN_HD ** -0.5, 1.0)
    for hh in range(n_heads):
        sl = slice(hh * DN_HD, (hh + 1) * DN_HD)
        yh = y[:, sl]
        nrm = yh * lax.rsqrt(jnp.sum(yh * yh, axis=-1, keepdims=True) + EPS) * q_scale
        yh = jnp.where(is_qk, nrm, yh)
        o_ref[:, sl] = jnp.where(valid, yh, 0.0).astype(o_ref.dtype)


def gdn_prep(z, conv_w, seq_rows, pad, tr):
    m = z.shape[0]
    w = DN_H * DN_HD
    hb = tr // HALO
    nhb = m // HALO
    assert seq_rows % tr == 0
    kern = functools.partial(_gdn_prep_kernel, tiles_per_seq=seq_rows // tr, n_heads=DN_H, pad=pad)
    return pl.pallas_call(
        kern,
        grid=(m // tr, 3),
        in_specs=[pl.BlockSpec((tr, w), lambda i, j: (i, j)),
                  pl.BlockSpec((HALO, w), lambda i, j: (jnp.maximum(i * hb - 1, 0), j)),
                  pl.BlockSpec((HALO, w), lambda i, j: (jnp.minimum((i + 1) * hb, nhb - 1), j)),
                  pl.BlockSpec((3, w), lambda i, j: (0, j))],
        out_specs=pl.BlockSpec((tr, w), lambda i, j: (i, j)),
        out_shape=jax.ShapeDtypeStruct((m, 3 * w), BF16),
        compiler_params=_cparams(("parallel", "arbitrary")),
        name="gdn_prep",
    )(z, z, z, conv_w)


def _gdn_gates_kernel(z_ref, p_ref, o_ref, *, tiles_per_seq, pad):
    i = pl.program_id(0)
    z = z_ref[...]
    tr = z.shape[0]
    a_neg_exp = p_ref[0:1, :]
    dt_bias = p_ref[1:2, :]
    kind = p_ref[2:3, :]
    x = z + dt_bias
    softplus = jnp.maximum(x, 0.0) + jnp.log(1.0 + jnp.exp(-jnp.abs(x)))
    g = a_neg_exp * softplus
    beta = 1.0 / (1.0 + jnp.exp(-z))
    out = jnp.where(kind == 1.0, g, jnp.where(kind == 2.0, beta, 0.0))
    valid = _valid_rows(i, tiles_per_seq, tr, pad)
    o_ref[...] = jnp.where(valid, out, 0.0)


def gdn_gates(z_ab, params, seq_rows, pad, tr):
    m, n = z_ab.shape
    assert seq_rows % tr == 0
    kern = functools.partial(_gdn_gates_kernel, tiles_per_seq=seq_rows // tr, pad=pad)
    return pl.pallas_call(
        kern,
        grid=(m // tr,),
        in_specs=[pl.BlockSpec((tr, n), lambda i: (i, 0)),
                  pl.BlockSpec((3, n), lambda i: (0, 0))],
        out_specs=pl.BlockSpec((tr, n), lambda i: (i, 0)),
        out_shape=jax.ShapeDtypeStruct((m, n), F32),
        compiler_params=_cparams(("parallel",)),
        name="gdn_gates",
    )(z_ab, params)


def _unit_triangular_inverses(a_mats, row, col):
    c = a_mats[0].shape[0]
    eye = (row == col).astype(F32)
    same = (row // INV_BASE) == (col // INV_BASE)
    diag = [jnp.where(same, a, 0.0) for a in a_mats]
    t_mats = [eye - b for b in diag]
    n_lvl = int(math.log2(INV_BASE)) - 1
    if n_lvl > 0:
        b_bf = [b.astype(BF16) for b in diag]
        p_mats = [_dot(b, b) for b in b_bf]
    for lvl in range(n_lvl):
        p_bf = [p.astype(BF16) for p in p_mats]
        if lvl < n_lvl - 1:
            xs = [_dot(jnp.concatenate([t.astype(BF16), p], axis=0), p) for t, p in zip(t_mats, p_bf)]
            t_mats = [t + x[:c] for t, x in zip(t_mats, xs)]
            p_mats = [x[c:] for x in xs]
        else:
            t_mats = [t + _dot(t.astype(BF16), p) for t, p in zip(t_mats, p_bf)]
    size = INV_BASE
    while size < c:
        off = ((row // (2 * size)) == (col // (2 * size))) & ((row // size) != (col // size))
        t_bf = [t.astype(BF16) for t in t_mats]
        ys = [_dot(jnp.where(off, a, 0.0).astype(BF16), t) for a, t in zip(a_mats, t_bf)]
        t_mats = [t - _dot(tb, y.astype(BF16)) for t, tb, y in zip(t_mats, t_bf, ys)]
        size *= 2
    return t_mats


def _gdn_scan_kernel(qf_ref, kf_ref, vf_ref, gf_ref, qb_ref, kb_ref, vb_ref, gb_ref,
                     of_ref, ob_ref, s_ref):
    @pl.when(pl.program_id(2) == 0)
    def _():
        s_ref[...] = jnp.zeros_like(s_ref)

    c = GCH
    d = DN_HD
    row = lax.broadcasted_iota(jnp.int32, (c, c), 0)
    col = lax.broadcasted_iota(jnp.int32, (c, c), 1)
    q, k, v, g_col, g_row, g_last, beta, incl, strict, outs = ([] for _ in range(10))
    for direction, (q_ref, k_ref, v_ref, g_ref, o_ref) in enumerate(
            ((qf_ref, kf_ref, vf_ref, gf_ref, of_ref), (qb_ref, kb_ref, vb_ref, gb_ref, ob_ref))):
        reverse = direction == 1
        gates = g_ref[...]
        mask_incl = (row <= col) if reverse else (row >= col)
        cum = jnp.dot(mask_incl.astype(F32), gates, preferred_element_type=F32,
                      precision=lax.Precision.HIGHEST)
        cum_t = cum.T
        for hh in range(HPG):
            sl = slice(hh * d, (hh + 1) * d)
            lane_g = direction * HPG + hh
            lane_b = 2 * HPG + lane_g
            q.append(q_ref[:, sl])
            k.append(k_ref[:, sl])
            v.append(v_ref[:, sl])
            gc = cum[:, lane_g:lane_g + 1]
            g_col.append(gc)
            g_row.append(cum_t[lane_g:lane_g + 1, :])
            g_last.append(gc[0:1, :] if reverse else gc[c - 1:c, :])
            beta.append(gates[:, lane_b:lane_b + 1])
            incl.append(mask_incl)
            strict.append((row < col) if reverse else (row > col))
            outs.append((o_ref, sl))
    n = len(q)
    rng = range(n)
    decay = [jnp.where(incl[i], jnp.exp(jnp.where(incl[i], g_col[i] - g_row[i], 0.0)), 0.0) for i in rng]
    kf = [k[i].astype(F32) for i in rng]
    kbeta = [kf[i] * beta[i] for i in rng]
    e_g = [jnp.exp(g_col[i]) for i in rng]
    kk = [_dot_nt(jnp.concatenate([kbeta[i].astype(BF16), q[i]], axis=0), k[i]) for i in rng]
    a_mats = [jnp.where(strict[i], kk[i][:c] * decay[i], 0.0) for i in rng]
    attn = [(kk[i][c:] * decay[i]).astype(BF16) for i in rng]
    t_mats = _unit_triangular_inverses(a_mats, row, col)
    rhs = [jnp.concatenate([(v[i].astype(F32) * beta[i]).astype(BF16),
                            (kbeta[i] * e_g[i]).astype(BF16)], axis=1) for i in rng]
    uw = [_dot(t_mats[i].astype(BF16), rhs[i]) for i in rng]
    s = [s_ref[i] for i in rng]
    qs = [_dot(jnp.concatenate([(q[i].astype(F32) * e_g[i]).astype(BF16), uw[i][:, d:].astype(BF16)], axis=0),
               s[i].astype(BF16)) for i in rng]
    v_new = [(uw[i][:, :d] - qs[i][c:]).astype(BF16) for i in rng]
    o = [qs[i][:c] + _dot(attn[i], v_new[i]) for i in rng]
    k_dec = [(kf[i] * jnp.exp(g_last[i] - g_col[i])).astype(BF16) for i in rng]
    s_new = [s[i] * jnp.exp(g_last[i]) + _dot_tn(k_dec[i], v_new[i]) for i in rng]
    for i in rng:
        s_ref[i] = s_new[i]
        o_ref, sl = outs[i]
        o_ref[:, sl] = o[i].astype(o_ref.dtype)


def gdn_scan(qkv, gates, batch, seq_rows):
    m = qkv.shape[0]
    w = DN_H * DN_HD
    ng = DN_H // HPG
    gw = HPG * DN_HD
    nb = seq_rows // GCH

    def fwd(colblk):
        return lambda b, g, s: (b * nb + s, colblk * ng + g)

    def bwd(colblk):
        return lambda b, g, s: (b * nb + nb - 1 - s, colblk * ng + g)

    blk = (GCH, gw)
    return pl.pallas_call(
        _gdn_scan_kernel,
        grid=(batch, ng, nb),
        in_specs=[pl.BlockSpec(blk, fwd(0)), pl.BlockSpec(blk, fwd(1)), pl.BlockSpec(blk, fwd(2)),
                  pl.BlockSpec((GCH, LANES), lambda b, g, s: (b * nb + s, g)),
                  pl.BlockSpec(blk, bwd(0)), pl.BlockSpec(blk, bwd(1)), pl.BlockSpec(blk, bwd(2)),
                  pl.BlockSpec((GCH, LANES), lambda b, g, s: (b * nb + nb - 1 - s, g))],
        out_specs=[pl.BlockSpec(blk, fwd(0)), pl.BlockSpec(blk, bwd(0))],
        out_shape=[jax.ShapeDtypeStruct((m, w), BF16), jax.ShapeDtypeStruct((m, w), BF16)],
        scratch_shapes=[pltpu.VMEM((2 * HPG, DN_HD, DN_HD), F32)],
        compiler_params=_cparams(("parallel", "parallel", "arbitrary")),
        name="gdn_scan",
    )(qkv, qkv, qkv, gates, qkv, qkv, qkv, gates)


def _gdn_post_kernel(of_ref, ob_ref, gate_ref, g_ref, o_ref, *, n_heads):
    gain = g_ref[...]
    for hh in range(n_heads):
        sl = slice(hh * DN_HD, (hh + 1) * DN_HD)
        o = of_ref[:, sl].astype(F32) + ob_ref[:, sl].astype(F32)
        y = o * lax.rsqrt(jnp.mean(o * o, axis=-1, keepdims=True) + EPS) * gain
        o_ref[:, sl] = (y * _silu(gate_ref[:, sl].astype(F32))).astype(o_ref.dtype)


def gdn_post(o_f, o_b, z, gate_colblk, gain, tr):
    m, w = o_f.shape
    kern = functools.partial(_gdn_post_kernel, n_heads=DN_H)
    return pl.pallas_call(
        kern,
        grid=(m // tr,),
        in_specs=[pl.BlockSpec((tr, w), lambda i: (i, 0)),
                  pl.BlockSpec((tr, w), lambda i: (i, 0)),
                  pl.BlockSpec((tr, w), lambda i: (i, gate_colblk)),
                  pl.BlockSpec((1, DN_HD), lambda i: (0, 0))],
        out_specs=pl.BlockSpec((tr, w), lambda i: (i, 0)),
        out_shape=jax.ShapeDtypeStruct((m, w), BF16),
        compiler_params=_cparams(("parallel",)),
        name="gdn_post",
    )(o_f, o_b, z, gain.reshape(1, DN_HD))


def _qk_norm_kernel(x_ref, g_ref, o_ref, *, n_groups):
    j = pl.program_id(1)
    gain = g_ref[0]
    scale = jnp.where(j == 0, DF_HD ** -0.5 * LOG2E, 1.0)
    for gi in range(n_groups):
        sl = slice(gi * DF_HD, (gi + 1) * DF_HD)
        x = x_ref[:, sl].astype(F32)
        y = x * lax.rsqrt(jnp.mean(x * x, axis=-1, keepdims=True) + EPS) * gain
        o_ref[:, sl] = (y * scale).astype(o_ref.dtype)


def qk_norm(z, q_colblk, gains, tr):
    m = z.shape[0]
    w = DF_H * 2 * DF_HD
    kern = functools.partial(_qk_norm_kernel, n_groups=DF_H * 2)
    return pl.pallas_call(
        kern,
        grid=(m // tr, 2),
        in_specs=[pl.BlockSpec((tr, w), lambda i, j: (i, q_colblk + j)),
                  pl.BlockSpec((1, 1, DF_HD), lambda i, j: (j, 0, 0))],
        out_specs=pl.BlockSpec((tr, w), lambda i, j: (i, j)),
        out_shape=jax.ShapeDtypeStruct((m, 2 * w), BF16),
        compiler_params=_cparams(("parallel", "arbitrary")),
        name="qk_norm",
    )(z, gains.reshape(2, 1, DF_HD))


def _attn_kernel(slopes_ref, q_ref, k_ref, v_ref, lp_ref, sg_ref, o_ref,
                 acc_ref, m_ref, l_ref, *, tk, lambda_init, pad):
    h = pl.program_id(1)
    qi = pl.program_id(2)
    tq = q_ref.shape[0]
    nk = k_ref.shape[0] // tk
    slope2 = slopes_ref[h] * LOG2E
    q = q_ref[...]
    q_maps = (q[:, :DF_HD], q[:, DF_HD:])
    q_row = (qi * tq + lax.broadcasted_iota(jnp.int32, (tq, 1), 0)).astype(F32)
    key_off = lax.broadcasted_iota(jnp.int32, (1, tk), 1).astype(F32)
    acc_ref[...] = jnp.zeros_like(acc_ref)
    m_ref[...] = jnp.full_like(m_ref, -1e30)
    l_ref[...] = jnp.zeros_like(l_ref)

    def chunk(c, mask_pad):
        k0 = pl.multiple_of(c * tk, tk)
        kc = k_ref[pl.ds(k0, tk), :]
        vc = v_ref[pl.ds(k0, tk), :]
        k0f = k0.astype(F32)
        mixed = c == (qi * tq) // tk

        def run(key_term, row_term):
            for mp in range(2):
                x = _dot_nt(q_maps[mp], kc[:, mp * DF_HD:(mp + 1) * DF_HD]) + key_term
                if mask_pad:
                    x = jnp.where(key_off >= float(pad), x, -1e30)
                m_old = m_ref[mp]
                m_loc = jnp.max(x, axis=-1, keepdims=True)
                if row_term is not None:
                    m_loc = m_loc + row_term
                m_new = jnp.maximum(m_old, m_loc)
                alpha = jnp.exp2(m_old - m_new)
                p = jnp.exp2(x - (m_new if row_term is None else m_new - row_term))
                l_ref[mp] = alpha * l_ref[mp] + jnp.sum(p, axis=-1, keepdims=True)
                m_ref[mp] = m_new
                acc_ref[mp] = alpha * acc_ref[mp] + _dot(p.astype(BF16), vc)

        @pl.when(mixed)
        def _():
            run(-slope2 * jnp.abs(q_row - (k0f + key_off)), None)

        @pl.when(jnp.logical_not(mixed))
        def _():
            sgn = jnp.where(k0 < qi * tq, slope2, -slope2)
            run(sgn * key_off, -sgn * (q_row - k0f))

    chunk(jnp.int32(0), True)

    def body(c, carry):
        chunk(c, False)
        return carry

    lax.fori_loop(1, nk, body, 0)

    lp = lp_ref[...]
    lam = (jnp.exp(jnp.sum(lp[0:1] * lp[1:2], axis=-1, keepdims=True))
           - jnp.exp(jnp.sum(lp[2:3] * lp[3:4], axis=-1, keepdims=True)) + lambda_init)
    o = acc_ref[0] / l_ref[0] - lam * (acc_ref[1] / l_ref[1])
    y = o * lax.rsqrt(jnp.mean(o * o, axis=-1, keepdims=True) + EPS) * sg_ref[...]
    y = y * (1.0 - lambda_init)
    o_ref[...] = jnp.where(q_row >= float(pad), y, 0.0).astype(o_ref.dtype)


def diff_attention(qk, z, v_colblk, slopes, lam_p, subln_g, lambda_init, batch, seq_rows, pad, tq, tk):
    m = qk.shape[0]
    nq = seq_rows // tq
    hw = 2 * DF_HD
    assert tk % tq == 0 and seq_rows % tk == 0 and tk > pad and seq_rows % tq == 0
    kern = functools.partial(_attn_kernel, tk=tk, lambda_init=lambda_init, pad=pad)
    return pl.pallas_call(
        kern,
        grid_spec=pltpu.PrefetchScalarGridSpec(
            num_scalar_prefetch=1,
            grid=(batch, DF_H, nq),
            in_specs=[pl.BlockSpec((tq, hw), lambda b, h, i, s: (b * nq + i, h)),
                      pl.BlockSpec((seq_rows, hw), lambda b, h, i, s: (b, DF_H + h)),
                      pl.BlockSpec((seq_rows, DF_VD), lambda b, h, i, s: (b, v_colblk + h)),
                      pl.BlockSpec((4, DF_HD), lambda b, h, i, s: (0, 0)),
                      pl.BlockSpec((1, DF_VD), lambda b, h, i, s: (0, 0))],
            out_specs=pl.BlockSpec((tq, DF_VD), lambda b, h, i, s: (b * nq + i, h)),
            scratch_shapes=[pltpu.VMEM((2, tq, DF_VD), F32),
                            pltpu.VMEM((2, tq, 1), F32),
                            pltpu.VMEM((2, tq, 1), F32)]),
        out_shape=jax.ShapeDtypeStruct((m, DF_H * DF_VD), BF16),
        compiler_params=_cparams(("parallel", "parallel", "arbitrary")),
        name="diff_attention",
    )(slopes, qk, qk, z, lam_p, subln_g.reshape(1, DF_VD))


def _prep_layer_params(l, w_in, dn_A_log, dn_dt_bias, w_out, w_gate, w_up, w_down):
    dn_w = DN_H * DN_HD
    df_w = DF_H * DF_VD
    n_ab = 2 * DN_H
    ab0 = 4 * dn_w
    w = w_in[l]
    w_main = jnp.concatenate([w[:, :ab0], w[:, ab0 + 2 * n_ab:]], axis=1).astype(BF16)
    wa = w[:, ab0:ab0 + n_ab].reshape(-1, 2, DN_H // HPG, HPG)
    wb = w[:, ab0 + n_ab:ab0 + 2 * n_ab].reshape(-1, 2, DN_H // HPG, HPG)
    lanes = jnp.concatenate([wa.transpose(0, 2, 1, 3), wb.transpose(0, 2, 1, 3)], axis=2)
    lanes = lanes.reshape(w.shape[0], DN_H // HPG, 4 * HPG)
    w_ab = jnp.pad(lanes, ((0, 0), (0, 0), (0, LANES - 4 * HPG))).reshape(w.shape[0], -1).astype(BF16)

    def gate_lanes(p):
        p = p.reshape(2, DN_H // HPG, HPG).transpose(1, 0, 2).reshape(DN_H // HPG, 2 * HPG)
        return jnp.pad(p, ((0, 0), (0, LANES - 2 * HPG))).reshape(1, -1)

    kind = jnp.concatenate([jnp.full((2 * HPG,), 1.0, F32), jnp.full((2 * HPG,), 2.0, F32),
                            jnp.zeros((LANES - 4 * HPG,), F32)])
    kind = jnp.tile(kind, DN_H // HPG).reshape(1, -1)
    gate_params = jnp.concatenate([gate_lanes(-jnp.exp(dn_A_log[l].astype(F32))),
                                   gate_lanes(dn_dt_bias[l].astype(F32)), kind], axis=0)
    return dict(w_main=w_main, w_ab=w_ab, gate_params=gate_params,
                w_out=w_out[l].astype(BF16), w_gate=w_gate[l].astype(BF16),
                w_up=w_up[l].astype(BF16), w_down=w_down[l].astype(BF16))


def _pick(m, cands):
    for c in cands:
        if m % c == 0:
            return c
    raise ValueError(f"no tile for {m}")


def _front_rows(seq):
    for front in FRONTS:
        rows = front + seq
        if rows <= ATTN_ONE_CHUNK or rows % ATTN_TILE == 0:
            return front
    raise ValueError(f"no row layout for sequence length {seq}")


def _attn_tiles(seq_rows):
    if seq_rows > ATTN_ONE_CHUNK:
        return ATTN_TILE, ATTN_TILE
    packed_rows = 16
    tq = max(t for t in range(packed_rows, ATTN_TILE + 1, packed_rows) if seq_rows % t == 0)
    return tq, seq_rows


def _trunk(x, meta_tokens, layers, norms, final_g):
    batch, seq, d = x.shape
    front = _front_rows(seq)
    pad = front - N_META
    seq_rows = front + seq
    m = batch * seq_rows
    meta = jnp.broadcast_to(meta_tokens.astype(x.dtype)[None], (batch, N_META, d))
    h = jnp.concatenate([jnp.zeros((batch, pad, d), x.dtype), meta, x], axis=1).reshape(m, d)

    dn_w = DN_H * DN_HD
    df_qw = DF_H * 2 * DF_HD
    tm = _pick(m, (768, 1024, 512, 256, 128))
    big_rows = tm > 768
    tr = _pick(seq_rows, (256, 128))
    tq, tk = _attn_tiles(seq_rows)
    slopes = jnp.exp2(-8.0 * (jnp.arange(DF_H, dtype=F32) + 1.0) / DF_H)
    n_main = 4 * dn_w + 2 * df_qw + DF_H * DF_VD
    tn_main = _pick(n_main, (1024, 512, 256))

    for l, (p, nm) in enumerate(zip(layers, norms)):
        lambda_init = 0.8 - 0.6 * math.exp(-0.3 * l)
        u = rmsnorm_rows(h, nm["attn_norm"], tr)
        z = matmul(u, p["w_main"], tm, tn_main, BF16)
        z_ab = matmul(u, p["w_ab"], tm, p["w_ab"].shape[1], F32)
        qkv = gdn_prep(z, nm["dn_conv_w"], seq_rows, pad, tr)
        gates = gdn_gates(z_ab, p["gate_params"], seq_rows, pad, tr)
        o_f, o_b = gdn_scan(qkv, gates, batch, seq_rows)
        o_dn = gdn_post(o_f, o_b, z, 3, nm["dn_norm"], tr)
        qk = qk_norm(z, (4 * dn_w) // df_qw, jnp.stack([nm["df_q_norm"], nm["df_k_norm"]]), tr)
        o_df = diff_attention(qk, z, (4 * dn_w + 2 * df_qw) // DF_VD, slopes, nm["df_lambda"],
                              nm["df_subln"], lambda_init, batch, seq_rows, pad, tq, tk)
        h = outproj_residual(o_dn, o_df, p["w_out"], h, tm,
                             _pick(d, (512, 256) if big_rows else (1024, 512, 256)))
        u = rmsnorm_rows(h, nm["ffn_norm"], tr)
        f = p["w_gate"].shape[1]
        act = ffn_gate_up(u, p["w_gate"], p["w_up"], nm["ffn_conv_w"], nm["ffn_conv_b"],
                          _pick(m, (FFN_ROW_SPLIT * tm, tm)), _pick(f, (256, 128)))
        nkb = 2 if (f // 2) % LANES == 0 else 1
        for kb in range(nkb):
            h = matmul_residual(act, p["w_down"], h, tm, _pick(d, (256,) if big_rows else (512, 256)), kb, nkb)
    return final_norm_rows(h, final_g, batch, seq, front)


def kernel(x_prompt, x_sample, meta_tokens, attn_norm, w_in, dn_conv_w, dn_A_log, dn_dt_bias, dn_norm, df_q_norm, df_k_norm, df_lambda, df_subln, w_out, ffn_norm, w_gate, w_up, ffn_conv_w, ffn_conv_b, w_down, final_norm):
    depth = w_in.shape[0]
    layers = [_prep_layer_params(l, w_in, dn_A_log, dn_dt_bias, w_out, w_gate, w_up, w_down)
              for l in range(depth)]
    norms = [dict(attn_norm=attn_norm[l], dn_conv_w=dn_conv_w[l], dn_norm=dn_norm[l],
                  df_q_norm=df_q_norm[l], df_k_norm=df_k_norm[l], df_lambda=df_lambda[l],
                  df_subln=df_subln[l], ffn_norm=ffn_norm[l], ffn_conv_w=ffn_conv_w[l],
                  ffn_conv_b=ffn_conv_b[l]) for l in range(depth)]
    y_prompt = _trunk(x_prompt, meta_tokens, layers, norms, final_norm)
    y_sample = _trunk(x_sample, meta_tokens, layers, norms, final_norm)
    return (y_prompt, y_sample)
```

```python
import functools
import math

import jax
import jax.numpy as jnp
from jax import lax
from jax.experimental import pallas as pl
from jax.experimental.pallas import tpu as pltpu

N_META = 16
DN_H = 16
DN_HD = 128
DF_H = 8
DF_HD = 128
DF_VD = 2 * DF_HD
EPS = 1e-6
LOG2E = math.log2(math.e)

FRONTS = (128, 256)
ATTN_TILE = 768
ATTN_ONE_CHUNK = 4096
GCH = 128
INV_BASE = 32
HPG = 16
FFN_ROW_SPLIT = 2
LANES = 128
HALO = 16
VMEM_LIMIT = 48 * 1024 * 1024

F32 = jnp.float32
BF16 = jnp.bfloat16


def _cparams(sem):
    return pltpu.CompilerParams(dimension_semantics=sem, vmem_limit_bytes=VMEM_LIMIT)


def _dot(a, b):
    return jnp.dot(a, b, preferred_element_type=F32)


def _dot_nt(a, b):
    return lax.dot_general(a, b, (((1,), (1,)), ((), ())), preferred_element_type=F32)


def _dot_tn(a, b):
    return lax.dot_general(a, b, (((0,), (0,)), ((), ())), preferred_element_type=F32)


def _silu(x):
    return x * (1.0 / (1.0 + jnp.exp(-x)))


def _rmsnorm_kernel(h_ref, g_ref, o_ref):
    x = h_ref[...]
    ms = jnp.mean(x * x, axis=-1, keepdims=True)
    o_ref[...] = (x * lax.rsqrt(ms + EPS) * g_ref[...]).astype(o_ref.dtype)


def rmsnorm_rows(h, g, tr):
    m, d = h.shape
    return pl.pallas_call(
        _rmsnorm_kernel,
        grid=(m // tr,),
        in_specs=[pl.BlockSpec((tr, d), lambda i: (i, 0)),
                  pl.BlockSpec((1, d), lambda i: (0, 0))],
        out_specs=pl.BlockSpec((tr, d), lambda i: (i, 0)),
        out_shape=jax.ShapeDtypeStruct((m, d), BF16),
        compiler_params=_cparams(("parallel",)),
        name="rmsnorm_rows",
    )(h, g.reshape(1, d))


def _final_norm_kernel(h_ref, g_ref, o_ref):
    x = h_ref[...]
    ms = jnp.mean(x * x, axis=-1, keepdims=True)
    o_ref[0] = x * lax.rsqrt(ms + EPS) * g_ref[...]


def final_norm_rows(h, g, batch, seq, front):
    m, d = h.shape
    tr = LANES
    nt = seq // tr
    nb = (front + seq) // tr
    off = front // tr
    return pl.pallas_call(
        _final_norm_kernel,
        grid=(batch, nt),
        in_specs=[pl.BlockSpec((tr, d), lambda b, i: (b * nb + off + i, 0)),
                  pl.BlockSpec((1, d), lambda b, i: (0, 0))],
        out_specs=pl.BlockSpec((1, tr, d), lambda b, i: (b, i, 0)),
        out_shape=jax.ShapeDtypeStruct((batch, seq, d), F32),
        compiler_params=_cparams(("parallel", "parallel")),
        name="final_norm",
    )(h, g.reshape(1, d))


def _matmul_kernel(x_ref, w_ref, o_ref):
    o_ref[...] = _dot(x_ref[...], w_ref[...]).astype(o_ref.dtype)


def matmul(x, w, tm, tn, out_dtype):
    m, k = x.shape
    n = w.shape[1]
    return pl.pallas_call(
        _matmul_kernel,
        grid=(m // tm, n // tn),
        in_specs=[pl.BlockSpec((tm, k), lambda i, j: (i, 0)),
                  pl.BlockSpec((k, tn), lambda i, j: (0, j))],
        out_specs=pl.BlockSpec((tm, tn), lambda i, j: (i, j)),
        out_shape=jax.ShapeDtypeStruct((m, n), out_dtype),
        compiler_params=_cparams(("parallel", "arbitrary")),
        name="matmul",
    )(x, w)


def _matmul_res_kernel(x_ref, w_ref, h_ref, o_ref):
    o_ref[...] = h_ref[...] + _dot(x_ref[...], w_ref[...])


def matmul_residual(x, w, h, tm, tn, kblk, nkb):
    m = x.shape[0]
    kb = x.shape[1] // nkb
    n = w.shape[1]
    return pl.pallas_call(
        _matmul_res_kernel,
        grid=(m // tm, n // tn),
        in_specs=[pl.BlockSpec((tm, kb), lambda i, j: (i, kblk)),
                  pl.BlockSpec((kb, tn), lambda i, j: (kblk, j)),
                  pl.BlockSpec((tm, tn), lambda i, j: (i, j))],
        out_specs=pl.BlockSpec((tm, tn), lambda i, j: (i, j)),
        out_shape=jax.ShapeDtypeStruct((m, n), F32),
        input_output_aliases={2: 0},
        compiler_params=_cparams(("parallel", "arbitrary")),
        name="matmul_residual",
    )(x, w, h)


def _outproj_kernel(x1_ref, x2_ref, w1_ref, w2_ref, h_ref, o_ref):
    o_ref[...] = h_ref[...] + (_dot(x1_ref[...], w1_ref[...]) + _dot(x2_ref[...], w2_ref[...]))


def outproj_residual(x1, x2, w, h, tm, tn):
    m, k1 = x1.shape
    k2 = x2.shape[1]
    assert k1 == k2 and w.shape[0] == k1 + k2
    n = w.shape[1]
    return pl.pallas_call(
        _outproj_kernel,
        grid=(m // tm, n // tn),
        in_specs=[pl.BlockSpec((tm, k1), lambda i, j: (i, 0)),
                  pl.BlockSpec((tm, k2), lambda i, j: (i, 0)),
                  pl.BlockSpec((k1, tn), lambda i, j: (0, j)),
                  pl.BlockSpec((k2, tn), lambda i, j: (1, j)),
                  pl.BlockSpec((tm, tn), lambda i, j: (i, j))],
        out_specs=pl.BlockSpec((tm, tn), lambda i, j: (i, j)),
        out_shape=jax.ShapeDtypeStruct((m, n), F32),
        input_output_aliases={4: 0},
        compiler_params=_cparams(("parallel", "arbitrary")),
        name="outproj_residual",
    )(x1, x2, w, w, h)


def _shift_rows(x, prev_row, next_row):
    t = x.shape[0]
    row = lax.broadcasted_iota(jnp.int32, x.shape, 0)
    xp = jnp.where(row == 0, prev_row, pltpu.roll(x, 1, axis=0))
    xn = jnp.where(row == t - 1, next_row, pltpu.roll(x, t - 1, axis=0))
    return xp, xn


def _ffn_gate_up_kernel(x_ref, xp_ref, xn_ref, wg_ref, wu_ref, cw_ref, cb_ref, o_ref):
    i = pl.program_id(0)
    ni = pl.num_programs(0)
    wg = wg_ref[...]
    wu = wu_ref[...]
    cw = cw_ref[...]
    cb = cb_ref[...]
    tm = x_ref.shape[0]
    ts = tm // FFN_ROW_SPLIT
    g = [_dot(x_ref[r * ts:(r + 1) * ts, :], wg) for r in range(FFN_ROW_SPLIT)]
    gp = _dot(xp_ref[...], wg)[HALO - 1:HALO, :] * jnp.where(i > 0, 1.0, 0.0)
    gn = _dot(xn_ref[...], wg)[0:1, :] * jnp.where(i < ni - 1, 1.0, 0.0)
    for r in range(FFN_ROW_SPLIT):
        prev_row = gp if r == 0 else g[r - 1][ts - 1:ts, :]
        next_row = gn if r == FFN_ROW_SPLIT - 1 else g[r + 1][0:1, :]
        g_prev, g_next = _shift_rows(g[r], prev_row, next_row)
        gt = g_prev * cw[0:1, :] + g[r] * cw[1:2, :] + g_next * cw[2:3, :] + cb
        up = _dot(x_ref[r * ts:(r + 1) * ts, :], wu)
        o_ref[r * ts:(r + 1) * ts, :] = (_silu(gt) * up).astype(o_ref.dtype)


def ffn_gate_up(x, wg, wu, cw, cb, tm, tf):
    m, k = x.shape
    f = wg.shape[1]
    hb = tm // HALO
    nhb = m // HALO
    return pl.pallas_call(
        _ffn_gate_up_kernel,
        grid=(m // tm, f // tf),
        in_specs=[pl.BlockSpec((tm, k), lambda i, j: (i, 0)),
                  pl.BlockSpec((HALO, k), lambda i, j: (jnp.maximum(i * hb - 1, 0), 0)),
                  pl.BlockSpec((HALO, k), lambda i, j: (jnp.minimum((i + 1) * hb, nhb - 1), 0)),
                  pl.BlockSpec((k, tf), lambda i, j: (0, j)),
                  pl.BlockSpec((k, tf), lambda i, j: (0, j)),
                  pl.BlockSpec((3, tf), lambda i, j: (0, j)),
                  pl.BlockSpec((1, tf), lambda i, j: (0, j))],
        out_specs=pl.BlockSpec((tm, tf), lambda i, j: (i, j)),
        out_shape=jax.ShapeDtypeStruct((m, f), BF16),
        compiler_params=_cparams(("parallel", "arbitrary")),
        name="ffn_gate_up",
    )(x, x, x, wg, wu, cw, cb.reshape(1, f))


def _valid_rows(tile_idx, tiles_per_seq, tr, pad):
    local = lax.rem(tile_idx, tiles_per_seq) * tr
    row = local + lax.broadcasted_iota(jnp.int32, (tr, 1), 0)
    return row >= pad


def _gdn_prep_kernel(x_ref, xp_ref, xn_ref, cw_ref, o_ref, *, tiles_per_seq, n_heads, pad):
    i = pl.program_id(0)
    j = pl.program_id(1)
    ni = pl.num_programs(0)
    x = x_ref[...].astype(F32)
    tr = x.shape[0]
    prev_row = xp_ref[HALO - 1:HALO, :].astype(F32) * jnp.where(i > 0, 1.0, 0.0)
    next_row = xn_ref[0:1, :].astype(F32) * jnp.where(i < ni - 1, 1.0, 0.0)
    x_prev, x_next = _shift_rows(x, prev_row, next_row)
    cw = cw_ref[...]
    y = _silu(x_prev * cw[0:1, :] + x * cw[1:2, :] + x_next * cw[2:3, :])
    valid = _valid_rows(i, tiles_per_seq, tr, pad)
    is_qk = j < 2
    q_scale = jnp.where(j == 0, DN_HD ** -0.5, 1.0)
    for hh in range(n_heads):
        sl = slice(hh * DN_HD, (hh + 1) * DN_HD)
        yh = y[:, sl]
        nrm = yh * lax.rsqrt(jnp.sum(yh * yh, axis=-1, keepdims=True) + EPS) * q_scale
        yh = jnp.where(is_qk, nrm, yh)
        o_ref[:, sl] = jnp.where(valid, yh, 0.0).astype(o_ref.dtype)


def gdn_prep(z, conv_w, seq_rows, pad, tr):
    m = z.shape[0]
    w = DN_H * DN_HD
    hb = tr // HALO
    nhb = m // HALO
    assert seq_rows % tr == 0
    kern = functools.partial(_gdn_prep_kernel, tiles_per_seq=seq_rows // tr, n_heads=DN_H, pad=pad)
    return pl.pallas_call(
        kern,
        grid=(m // tr, 3),
        in_specs=[pl.BlockSpec((tr, w), lambda i, j: (i, j)),
                  pl.BlockSpec((HALO, w), lambda i, j: (jnp.maximum(i * hb - 1, 0), j)),
                  pl.BlockSpec((HALO, w), lambda i, j: (jnp.minimum((i + 1) * hb, nhb - 1), j)),
                  pl.BlockSpec((3, w), lambda i, j: (0, j))],
        out_specs=pl.BlockSpec((tr, w), lambda i, j: (i, j)),
        out_shape=jax.ShapeDtypeStruct((m, 3 * w), BF16),
        compiler_params=_cparams(("parallel", "arbitrary")),
        name="gdn_prep",
    )(z, z, z, conv_w)


def _gdn_gates_kernel(z_ref, p_ref, o_ref, *, tiles_per_seq, pad):
    i = pl.program_id(0)
    z = z_ref[...]
    tr = z.shape[0]
    a_neg_exp = p_ref[0:1, :]
    dt_bias = p_ref[1:2, :]
    kind = p_ref[2:3, :]
    x = z + dt_bias
    softplus = jnp.maximum(x, 0.0) + jnp.log(1.0 + jnp.exp(-jnp.abs(x)))
    g = a_neg_exp * softplus
    beta = 1.0 / (1.0 + jnp.exp(-z))
    out = jnp.where(kind == 1.0, g, jnp.where(kind == 2.0, beta, 0.0))
    valid = _valid_rows(i, tiles_per_seq, tr, pad)
    o_ref[...] = jnp.where(valid, out, 0.0)


def gdn_gates(z_ab, params, seq_rows, pad, tr):
    m, n = z_ab.shape
    assert seq_rows % tr == 0
    kern = functools.partial(_gdn_gates_kernel, tiles_per_seq=seq_rows // tr, pad=pad)
    return pl.pallas_call(
        kern,
        grid=(m // tr,),
        in_specs=[pl.BlockSpec((tr, n), lambda i: (i, 0)),
                  pl.BlockSpec((3, n), lambda i: (0, 0))],
        out_specs=pl.BlockSpec((tr, n), lambda i: (i, 0)),
        out_shape=jax.ShapeDtypeStruct((m, n), F32),
        compiler_params=_cparams(("parallel",)),
        name="gdn_gates",
    )(z_ab, params)


def _unit_triangular_inverses(a_mats, row, col):
    c = a_mats[0].shape[0]
    eye = (row == col).astype(F32)
    same = (row // INV_BASE) == (col // INV_BASE)
    diag = [jnp.where(same, a, 0.0) for a in a_mats]
    t_mats = [eye - b for b in diag]
    n_lvl = int(math.log2(INV_BASE)) - 1
    if n_lvl > 0:
        b_bf = [b.astype(BF16) for b in diag]
        p_mats = [_dot(b, b) for b in b_bf]
    for lvl in range(n_lvl):
        p_bf = [p.astype(BF16) for p in p_mats]
        if lvl < n_lvl - 1:
            xs = [_dot(jnp.concatenate([t.astype(BF16), p], axis=0), p) for t, p in zip(t_mats, p_bf)]
            t_mats = [t + x[:c] for t, x in zip(t_mats, xs)]
            p_mats = [x[c:] for x in xs]
        else:
            t_mats = [t + _dot(t.astype(BF16), p) for t, p in zip(t_mats, p_bf)]
    size = INV_BASE
    while size < c:
        off = ((row // (2 * size)) == (col // (2 * size))) & ((row // size) != (col // size))
        t_bf = [t.astype(BF16) for t in t_mats]
        ys = [_dot(jnp.where(off, a, 0.0).astype(BF16), t) for a, t in zip(a_mats, t_bf)]
        t_mats = [t - _dot(tb, y.astype(BF16)) for t, tb, y in zip(t_mats, t_bf, ys)]
        size *= 2
    return t_mats


def _gdn_scan_kernel(qf_ref, kf_ref, vf_ref, gf_ref, qb_ref, kb_ref, vb_ref, gb_ref,
                     of_ref, ob_ref, s_ref):
    @pl.when(pl.program_id(2) == 0)
    def _():
        s_ref[...] = jnp.zeros_like(s_ref)

    c = GCH
    d = DN_HD
    row = lax.broadcasted_iota(jnp.int32, (c, c), 0)
    col = lax.broadcasted_iota(jnp.int32, (c, c), 1)
    q, k, v, g_col, g_row, g_last, beta, incl, strict, outs = ([] for _ in range(10))
    for direction, (q_ref, k_ref, v_ref, g_ref, o_ref) in enumerate(
            ((qf_ref, kf_ref, vf_ref, gf_ref, of_ref), (qb_ref, kb_ref, vb_ref, gb_ref, ob_ref))):
        reverse = direction == 1
        gates = g_ref[...]
        mask_incl = (row <= col) if reverse else (row >= col)
        cum = jnp.dot(mask_incl.astype(F32), gates, preferred_element_type=F32,
                      precision=lax.Precision.HIGHEST)
        cum_t = cum.T
        for hh in range(HPG):
            sl = slice(hh * d, (hh + 1) * d)
            lane_g = direction * HPG + hh
            lane_b = 2 * HPG + lane_g
            q.append(q_ref[:, sl])
            k.append(k_ref[:, sl])
            v.append(v_ref[:, sl])
            gc = cum[:, lane_g:lane_g + 1]
            g_col.append(gc)
            g_row.append(cum_t[lane_g:lane_g + 1, :])
            g_last.append(gc[0:1, :] if reverse else gc[c - 1:c, :])
            beta.append(gates[:, lane_b:lane_b + 1])
            incl.append(mask_incl)
            strict.append((row < col) if reverse else (row > col))
            outs.append((o_ref, sl))
    n = len(q)
    rng = range(n)
    decay = [jnp.where(incl[i], jnp.exp(jnp.where(incl[i], g_col[i] - g_row[i], 0.0)), 0.0) for i in rng]
    kf = [k[i].astype(F32) for i in rng]
    kbeta = [kf[i] * beta[i] for i in rng]
    e_g = [jnp.exp(g_col[i]) for i in rng]
    kk = [_dot_nt(jnp.concatenate([kbeta[i].astype(BF16), q[i]], axis=0), k[i]) for i in rng]
    a_mats = [jnp.where(strict[i], kk[i][:c] * decay[i], 0.0) for i in rng]
    attn = [(kk[i][c:] * decay[i]).astype(BF16) for i in rng]
    t_mats = _unit_triangular_inverses(a_mats, row, col)
    rhs = [jnp.concatenate([(v[i].astype(F32) * beta[i]).astype(BF16),
                            (kbeta[i] * e_g[i]).astype(BF16)], axis=1) for i in rng]
    uw = [_dot(t_mats[i].astype(BF16), rhs[i]) for i in rng]
    s = [s_ref[i] for i in rng]
    qs = [_dot(jnp.concatenate([(q[i].astype(F32) * e_g[i]).astype(BF16), uw[i][:, d:].astype(BF16)], axis=0),
               s[i].astype(BF16)) for i in rng]
    v_new = [(uw[i][:, :d] - qs[i][c:]).astype(BF16) for i in rng]
    o = [qs[i][:c] + _dot(attn[i], v_new[i]) for i in rng]
    k_dec = [(kf[i] * jnp.exp(g_last[i] - g_col[i])).astype(BF16) for i in rng]
    s_new = [s[i] * jnp.exp(g_last[i]) + _dot_tn(k_dec[i], v_new[i]) for i in rng]
    for i in rng:
        s_ref[i] = s_new[i]
        o_ref, sl = outs[i]
        o_ref[:, sl] = o[i].astype(o_ref.dtype)


def gdn_scan(qkv, gates, batch, seq_rows):
    m = qkv.shape[0]
    w = DN_H * DN_HD
    ng = DN_H // HPG
    gw = HPG * DN_HD
    nb = seq_rows // GCH

    def fwd(colblk):
        return lambda b, g, s: (b * nb + s, colblk * ng + g)

    def bwd(colblk):
        return lambda b, g, s: (b * nb + nb - 1 - s, colblk * ng + g)

    blk = (GCH, gw)
    return pl.pallas_call(
        _gdn_scan_kernel,
        grid=(batch, ng, nb),
        in_specs=[pl.BlockSpec(blk, fwd(0)), pl.BlockSpec(blk, fwd(1)), pl.BlockSpec(blk, fwd(2)),
                  pl.BlockSpec((GCH, LANES), lambda b, g, s: (b * nb + s, g)),
                  pl.BlockSpec(blk, bwd(0)), pl.BlockSpec(blk, bwd(1)), pl.BlockSpec(blk, bwd(2)),
                  pl.BlockSpec((GCH, LANES), lambda b, g, s: (b * nb + nb - 1 - s, g))],
        out_specs=[pl.BlockSpec(blk, fwd(0)), pl.BlockSpec(blk, bwd(0))],
        out_shape=[jax.ShapeDtypeStruct((m, w), BF16), jax.ShapeDtypeStruct((m, w), BF16)],
        scratch_shapes=[pltpu.VMEM((2 * HPG, DN_HD, DN_HD), F32)],
        compiler_params=_cparams(("parallel", "parallel", "arbitrary")),
        name="gdn_scan",
    )(qkv, qkv, qkv, gates, qkv, qkv, qkv, gates)


def _gdn_post_kernel(of_ref, ob_ref, gate_ref, g_ref, o_ref, *, n_heads):
    gain = g_ref[...]
    for hh in range(n_heads):
        sl = slice(hh * DN_HD, (hh + 1) * DN_HD)
        o = of_ref[:, sl].astype(F32) + ob_ref[:, sl].astype(F32)
        y = o * lax.rsqrt(jnp.mean(o * o, axis=-1, keepdims=True) + EPS) * gain
        o_ref[:, sl] = (y * _silu(gate_ref[:, sl].astype(F32))).astype(o_ref.dtype)


def gdn_post(o_f, o_b, z, gate_colblk, gain, tr):
    m, w = o_f.shape
    kern = functools.partial(_gdn_post_kernel, n_heads=DN_H)
    return pl.pallas_call(
        kern,
        grid=(m // tr,),
        in_specs=[pl.BlockSpec((tr, w), lambda i: (i, 0)),
                  pl.BlockSpec((tr, w), lambda i: (i, 0)),
                  pl.BlockSpec((tr, w), lambda i: (i, gate_colblk)),
                  pl.BlockSpec((1, DN_HD), lambda i: (0, 0))],
        out_specs=pl.BlockSpec((tr, w), lambda i: (i, 0)),
        out_shape=jax.ShapeDtypeStruct((m, w), BF16),
        compiler_params=_cparams(("parallel",)),
        name="gdn_post",
    )(o_f, o_b, z, gain.reshape(1, DN_HD))


def _qk_norm_kernel(x_ref, g_ref, o_ref, *, n_groups):
    j = pl.program_id(1)
    gain = g_ref[0]
    scale = jnp.where(j == 0, DF_HD ** -0.5 * LOG2E, 1.0)
    for gi in range(n_groups):
        sl = slice(gi * DF_HD, (gi + 1) * DF_HD)
        x = x_ref[:, sl].astype(F32)
        y = x * lax.rsqrt(jnp.mean(x * x, axis=-1, keepdims=True) + EPS) * gain
        o_ref[:, sl] = (y * scale).astype(o_ref.dtype)


def qk_norm(z, q_colblk, gains, tr):
    m = z.shape[0]
    w = DF_H * 2 * DF_HD
    kern = functools.partial(_qk_norm_kernel, n_groups=DF_H * 2)
    return pl.pallas_call(
        kern,
        grid=(m // tr, 2),
        in_specs=[pl.BlockSpec((tr, w), lambda i, j: (i, q_colblk + j)),
                  pl.BlockSpec((1, 1, DF_HD), lambda i, j: (j, 0, 0))],
        out_specs=pl.BlockSpec((tr, w), lambda i, j: (i, j)),
        out_shape=jax.ShapeDtypeStruct((m, 2 * w), BF16),
        compiler_params=_cparams(("parallel", "arbitrary")),
        name="qk_norm",
    )(z, gains.reshape(2, 1, DF_HD))


def _attn_kernel(slopes_ref, q_ref, k_ref, v_ref, lp_ref, sg_ref, o_ref,
                 acc_ref, m_ref, l_ref, *, tk, lambda_init, pad):
    h = pl.program_id(1)
    qi = pl.program_id(2)
    tq = q_ref.shape[0]
    nk = k_ref.shape[0] // tk
    slope2 = slopes_ref[h] * LOG2E
    q = q_ref[...]
    q_maps = (q[:, :DF_HD], q[:, DF_HD:])
    q_row = (qi * tq + lax.broadcasted_iota(jnp.int32, (tq, 1), 0)).astype(F32)
    key_off = lax.broadcasted_iota(jnp.int32, (1, tk), 1).astype(F32)
    acc_ref[...] = jnp.zeros_like(acc_ref)
    m_ref[...] = jnp.full_like(m_ref, -1e30)
    l_ref[...] = jnp.zeros_like(l_ref)

    def chunk(c, mask_pad):
        k0 = pl.multiple_of(c * tk, tk)
        kc = k_ref[pl.ds(k0, tk), :]
        vc = v_ref[pl.ds(k0, tk), :]
        k0f = k0.astype(F32)
        mixed = c == (qi * tq) // tk

        def run(key_term, row_term):
            for mp in range(2):
                x = _dot_nt(q_maps[mp], kc[:, mp * DF_HD:(mp + 1) * DF_HD]) + key_term
                if mask_pad:
                    x = jnp.where(key_off >= float(pad), x, -1e30)
                m_old = m_ref[mp]
                m_loc = jnp.max(x, axis=-1, keepdims=True)
                if row_term is not None:
                    m_loc = m_loc + row_term
                m_new = jnp.maximum(m_old, m_loc)
                alpha = jnp.exp2(m_old - m_new)
                p = jnp.exp2(x - (m_new if row_term is None else m_new - row_term))
                l_ref[mp] = alpha * l_ref[mp] + jnp.sum(p, axis=-1, keepdims=True)
                m_ref[mp] = m_new
                acc_ref[mp] = alpha * acc_ref[mp] + _dot(p.astype(BF16), vc)

        @pl.when(mixed)
        def _():
            run(-slope2 * jnp.abs(q_row - (k0f + key_off)), None)

        @pl.when(jnp.logical_not(mixed))
        def _():
            sgn = jnp.where(k0 < qi * tq, slope2, -slope2)
            run(sgn * key_off, -sgn * (q_row - k0f))

    chunk(jnp.int32(0), True)

    def body(c, carry):
        chunk(c, False)
        return carry

    lax.fori_loop(1, nk, body, 0)

    lp = lp_ref[...]
    lam = (jnp.exp(jnp.sum(lp[0:1] * lp[1:2], axis=-1, keepdims=True))
           - jnp.exp(jnp.sum(lp[2:3] * lp[3:4], axis=-1, keepdims=True)) + lambda_init)
    o = acc_ref[0] / l_ref[0] - lam * (acc_ref[1] / l_ref[1])
    y = o * lax.rsqrt(jnp.mean(o * o, axis=-1, keepdims=True) + EPS) * sg_ref[...]
    y = y * (1.0 - lambda_init)
    o_ref[...] = jnp.where(q_row >= float(pad), y, 0.0).astype(o_ref.dtype)


def diff_attention(qk, z, v_colblk, slopes, lam_p, subln_g, lambda_init, batch, seq_rows, pad, tq, tk):
    m = qk.shape[0]
    nq = seq_rows // tq
    hw = 2 * DF_HD
    assert tk % tq == 0 and seq_rows % tk == 0 and tk > pad and seq_rows % tq == 0
    kern = functools.partial(_attn_kernel, tk=tk, lambda_init=lambda_init, pad=pad)
    return pl.pallas_call(
        kern,
        grid_spec=pltpu.PrefetchScalarGridSpec(
            num_scalar_prefetch=1,
            grid=(batch, DF_H, nq),
            in_specs=[pl.BlockSpec((tq, hw), lambda b, h, i, s: (b * nq + i, h)),
                      pl.BlockSpec((seq_rows, hw), lambda b, h, i, s: (b, DF_H + h)),
                      pl.BlockSpec((seq_rows, DF_VD), lambda b, h, i, s: (b, v_colblk + h)),
                      pl.BlockSpec((4, DF_HD), lambda b, h, i, s: (0, 0)),
                      pl.BlockSpec((1, DF_VD), lambda b, h, i, s: (0, 0))],
            out_specs=pl.BlockSpec((tq, DF_VD), lambda b, h, i, s: (b * nq + i, h)),
            scratch_shapes=[pltpu.VMEM((2, tq, DF_VD), F32),
                            pltpu.VMEM((2, tq, 1), F32),
                            pltpu.VMEM((2, tq, 1), F32)]),
        out_shape=jax.ShapeDtypeStruct((m, DF_H * DF_VD), BF16),
        compiler_params=_cparams(("parallel", "parallel", "arbitrary")),
        name="diff_attention",
    )(slopes, qk, qk, z, lam_p, subln_g.reshape(1, DF_VD))


def _prep_layer_params(l, w_in, dn_A_log, dn_dt_bias, w_out, w_gate, w_up, w_down):
    dn_w = DN_H * DN_HD
    df_w = DF_H * DF_VD
    n_ab = 2 * DN_H
    ab0 = 4 * dn_w
    w = w_in[l]
    w_main = jnp.concatenate([w[:, :ab0], w[:, ab0 + 2 * n_ab:]], axis=1).astype(BF16)
    wa = w[:, ab0:ab0 + n_ab].reshape(-1, 2, DN_H // HPG, HPG)
    wb = w[:, ab0 + n_ab:ab0 + 2 * n_ab].reshape(-1, 2, DN_H // HPG, HPG)
    lanes = jnp.concatenate([wa.transpose(0, 2, 1, 3), wb.transpose(0, 2, 1, 3)], axis=2)
    lanes = lanes.reshape(w.shape[0], DN_H // HPG, 4 * HPG)
    w_ab = jnp.pad(lanes, ((0, 0), (0, 0), (0, LANES - 4 * HPG))).reshape(w.shape[0], -1).astype(BF16)

    def gate_lanes(p):
        p = p.reshape(2, DN_H // HPG, HPG).transpose(1, 0, 2).reshape(DN_H // HPG, 2 * HPG)
        return jnp.pad(p, ((0, 0), (0, LANES - 2 * HPG))).reshape(1, -1)

    kind = jnp.concatenate([jnp.full((2 * HPG,), 1.0, F32), jnp.full((2 * HPG,), 2.0, F32),
                            jnp.zeros((LANES - 4 * HPG,), F32)])
    kind = jnp.tile(kind, DN_H // HPG).reshape(1, -1)
    gate_params = jnp.concatenate([gate_lanes(-jnp.exp(dn_A_log[l].astype(F32))),
                                   gate_lanes(dn_dt_bias[l].astype(F32)), kind], axis=0)
    return dict(w_main=w_main, w_ab=w_ab, gate_params=gate_params,
                w_out=w_out[l].astype(BF16), w_gate=w_gate[l].astype(BF16),
                w_up=w_up[l].astype(BF16), w_down=w_down[l].astype(BF16))


def _pick(m, cands):
    for c in cands:
        if m % c == 0:
            return c
    raise ValueError(f"no tile for {m}")


def _front_rows(seq):
    for front in FRONTS:
        rows = front + seq
        if rows <= ATTN_ONE_CHUNK or rows % ATTN_TILE == 0:
            return front
    raise ValueError(f"no row layout for sequence length {seq}")


def _attn_tiles(seq_rows):
    if seq_rows > ATTN_ONE_CHUNK:
        return ATTN_TILE, ATTN_TILE
    packed_rows = 16
    tq = max(t for t in range(packed_rows, ATTN_TILE + 1, packed_rows) if seq_rows % t == 0)
    return tq, seq_rows


def _trunk(x, meta_tokens, layers, norms, final_g):
    batch, seq, d = x.shape
    front = _front_rows(seq)
    pad = front - N_META
    seq_rows = front + seq
    m = batch * seq_rows
    meta = jnp.broadcast_to(meta_tokens.astype(x.dtype)[None], (batch, N_META, d))
    h = jnp.concatenate([jnp.zeros((batch, pad, d), x.dtype), meta, x], axis=1).reshape(m, d)

    dn_w = DN_H * DN_HD
    df_qw = DF_H * 2 * DF_HD
    tm = _pick(m, (768, 1024, 512, 256, 128))
    big_rows = tm > 768
    tr = _pick(seq_rows, (256, 128))
    tq, tk = _attn_tiles(seq_rows)
    slopes = jnp.exp2(-8.0 * (jnp.arange(DF_H, dtype=F32) + 1.0) / DF_H)
    n_main = 4 * dn_w + 2 * df_qw + DF_H * DF_VD
    tn_main = _pick(n_main, (1024, 512, 256))

    for l, (p, nm) in enumerate(zip(layers, norms)):
        lambda_init = 0.8 - 0.6 * math.exp(-0.3 * l)
        u = rmsnorm_rows(h, nm["attn_norm"], tr)
        z = matmul(u, p["w_main"], tm, tn_main, BF16)
        z_ab = matmul(u, p["w_ab"], tm, p["w_ab"].shape[1], F32)
        qkv = gdn_prep(z, nm["dn_conv_w"], seq_rows, pad, tr)
        gates = gdn_gates(z_ab, p["gate_params"], seq_rows, pad, tr)
        o_f, o_b = gdn_scan(qkv, gates, batch, seq_rows)
        o_dn = gdn_post(o_f, o_b, z, 3, nm["dn_norm"], tr)
        qk = qk_norm(z, (4 * dn_w) // df_qw, jnp.stack([nm["df_q_norm"], nm["df_k_norm"]]), tr)
        o_df = diff_attention(qk, z, (4 * dn_w + 2 * df_qw) // DF_VD, slopes, nm["df_lambda"],
                              nm["df_subln"], lambda_init, batch, seq_rows, pad, tq, tk)
        h = outproj_residual(o_dn, o_df, p["w_out"], h, tm,
                             _pick(d, (512, 256) if big_rows else (1024, 512, 256)))
        u = rmsnorm_rows(h, nm["ffn_norm"], tr)
        f = p["w_gate"].shape[1]
        act = ffn_gate_up(u, p["w_gate"], p["w_up"], nm["ffn_conv_w"], nm["ffn_conv_b"],
                          _pick(m, (FFN_ROW_SPLIT * tm, tm)), _pick(f, (256, 128)))
        nkb = 2 if (f // 2) % LANES == 0 else 1
        for kb in range(nkb):
            h = matmul_residual(act, p["w_down"], h, tm, _pick(d, (256,) if big_rows else (512, 256)), kb, nkb)
    return final_norm_rows(h, final_g, batch, seq, front)


def kernel(x_prompt, x_sample, meta_tokens, attn_norm, w_in, dn_conv_w, dn_A_log, dn_dt_bias, dn_norm, df_q_norm, df_k_norm, df_lambda, df_subln, w_out, ffn_norm, w_gate, w_up, ffn_conv_w, ffn_conv_b, w_down, final_norm):
    depth = w_in.shape[0]
    layers = [_prep_layer_params(l, w_in, dn_A_log, dn_dt_bias, w_out, w_gate, w_up, w_down)
              for l in range(depth)]
    norms = [dict(attn_norm=attn_norm[l], dn_conv_w=dn_conv_w[l], dn_norm=dn_norm[l],
                  df_q_norm=df_q_norm[l], df_k_norm=df_k_norm[l], df_lambda=df_lambda[l],
                  df_subln=df_subln[l], ffn_norm=ffn_norm[l], ffn_conv_w=ffn_conv_w[l],
                  ffn_conv_b=ffn_conv_b[l]) for l in range(depth)]
    y_prompt = _trunk(x_prompt, meta_tokens, layers, norms, final_norm)
    y_sample = _trunk(x_sample, meta_tokens, layers, norms, final_norm)
    return (y_prompt, y_sample)
```
